```python
import jax, jax.numpy as jnp
from jax import lax
import numpy as np

D_MODEL = 1024
BATCH = 4
SEQ = 8192
DEPTH = 4

GRID_W = 64
CTX_LEN = 256
MIX_WIDTH = D_MODEL
ATTN_WIDTH = MIX_WIDTH // 2
POOL_WIDTH = MIX_WIDTH // 4
CONV_WIDTH = MIX_WIDTH - ATTN_WIDTH - POOL_WIDTH
NOPE_DIM = 128
ROPE_DIM = 64
V_DIM = 128
MLA_HEADS = ATTN_WIDTH // V_DIM
Q_RANK = D_MODEL // 4
KV_RANK = D_MODEL // 8
QK_DIM = NOPE_DIM + ROPE_DIM
SOFTMAX_SCALE = QK_DIM ** -0.5
ROPE_THETA = 10000.0
POOL_WINDOWS = (2, 4, 8, 16)
POOL_GROUPS = len(POOL_WINDOWS)
POOL_CH = POOL_WIDTH // POOL_GROUPS
CONV_K = 3
D_FF = ((8 * D_MODEL // 3 + 127) // 128) * 128
Q_BLOCK = 128
EPS = 1e-6
IN_SIZES = (Q_RANK, KV_RANK, ROPE_DIM, POOL_WIDTH, CONV_WIDTH, CONV_WIDTH, CONV_WIDTH)
IN_WIDTH = sum(IN_SIZES)
IN_SPLITS = tuple(int(s) for s in np.cumsum(IN_SIZES)[:-1])

kernel_name = 'hybrid_mla_pool_shortconv_dit'


def rmsnorm(x, g):
    xf = x.astype(jnp.float32)
    y = xf * lax.rsqrt(jnp.mean(xf * xf, axis=-1, keepdims=True) + EPS)
    return (y * g.astype(jnp.float32)).astype(x.dtype)


def adaln(cvec, w, b):
    m = jax.nn.silu(cvec) @ w + b
    m = m.reshape(-1, 1, 6 * D_MODEL)
    return jnp.split(m, 6, axis=-1)


def axial_rope_tables(n_tokens):
    rows = n_tokens // GRID_W
    row = jnp.repeat(jnp.arange(rows), GRID_W).astype(jnp.float32)
    col = jnp.tile(jnp.arange(GRID_W), rows).astype(jnp.float32)
    n_freq = ROPE_DIM // 4
    inv = ROPE_THETA ** (-jnp.arange(n_freq, dtype=jnp.float32) / n_freq)
    ar = row[:, None] * inv
    ac = col[:, None] * inv
    ang = jnp.concatenate([ar, ar, ac, ac], axis=-1)
    return jnp.cos(ang), jnp.sin(ang)


def apply_axial_rope(x, cos, sin):
    n_freq = ROPE_DIM // 4
    xr = x.reshape(x.shape[:-1] + (2, 2, n_freq))
    rot = jnp.stack([-xr[..., 1, :], xr[..., 0, :]], axis=-2).reshape(x.shape)
    return x * cos.astype(x.dtype) + rot * sin.astype(x.dtype)


def dwconv3(u, w):
    up = jnp.pad(u, ((0, 0), (1, 1), (0, 0)))
    return up[:, :-2] * w[0] + up[:, 1:-1] * w[1] + up[:, 2:] * w[2]


def mixer_features(h, w_in, qc_g, kvc_g, w_q_up, w_kv_up, qn_g, qr_g, kn_g, kr_g, rope):
    b, n, _ = h.shape
    q_lat, kv_lat, k_rope, pool_in, gate_b, gate_c, conv_in = jnp.split(h @ w_in, IN_SPLITS, axis=-1)
    q = (rmsnorm(q_lat, qc_g) @ w_q_up).reshape(b, n, MLA_HEADS, QK_DIM)
    kv = (rmsnorm(kv_lat, kvc_g) @ w_kv_up).reshape(b, n, MLA_HEADS, NOPE_DIM + V_DIM)
    q_nope = rmsnorm(q[..., :NOPE_DIM], qn_g)
    q_rope = rmsnorm(q[..., NOPE_DIM:], qr_g)
    k_nope = rmsnorm(kv[..., :NOPE_DIM], kn_g)
    v = kv[..., NOPE_DIM:]
    k_rope = rmsnorm(k_rope, kr_g)
    if rope is not None:
        cos, sin = rope
        q_rope = apply_axial_rope(q_rope, cos[:, None, :], sin[:, None, :])
        k_rope = apply_axial_rope(k_rope, cos, sin)
    attn_in = (q_nope * SOFTMAX_SCALE, q_rope * SOFTMAX_SCALE, k_nope, k_rope, v)
    return attn_in, (pool_in, gate_b, gate_c, conv_in)


def attend(q_nope, q_rope, k_nope, k_rope, v):
    s = jnp.einsum('bqhd,bkhd->bhqk', q_nope, k_nope) + jnp.einsum('bqhr,bkr->bhqk', q_rope, k_rope)
    p = jax.nn.softmax(s.astype(jnp.float32), axis=-1).astype(v.dtype)
    o = jnp.einsum('bhqk,bkhd->bqhd', p, v)
    return o.reshape(o.shape[0], o.shape[1], ATTN_WIDTH)


def blocked_attention(q_nope, q_rope, k_nope, k_rope, v):
    b, n = q_nope.shape[:2]
    nblk = n // Q_BLOCK
    qn = q_nope.reshape(b, nblk, Q_BLOCK, MLA_HEADS, NOPE_DIM).swapaxes(0, 1)
    qr = q_rope.reshape(b, nblk, Q_BLOCK, MLA_HEADS, ROPE_DIM).swapaxes(0, 1)
    o = lax.map(lambda qs: attend(qs[0], qs[1], k_nope, k_rope, v), (qn, qr))
    return o.swapaxes(0, 1).reshape(b, n, ATTN_WIDTH)


def multiscale_pool(p, w, scale):
    b, n, _ = p.shape
    cs = jnp.pad(jnp.cumsum(p.astype(jnp.float32), axis=1), ((0, 0), (1, 0), (0, 0)))
    t = jnp.arange(n)
    means = []
    for g, win in enumerate(POOL_WINDOWS):
        lo = jnp.clip(t - win // 2, 0, n)
        hi = jnp.clip(t + win // 2, 0, n)
        csg = cs[..., g * POOL_CH:(g + 1) * POOL_CH]
        total = jnp.take(csg, hi, axis=1) - jnp.take(csg, lo, axis=1)
        means.append(total / (hi - lo).astype(jnp.float32)[None, :, None])
    pooled = jnp.stack(means, axis=2).astype(p.dtype)
    diff = pooled - p.reshape(b, n, POOL_GROUPS, POOL_CH)
    y = jnp.einsum('bngc,gcd->bngd', diff, w).reshape(b, n, POOL_WIDTH)
    return y * scale


def mixer_output(attn, other, pool_w, pool_scale, sconv_w, w_o):
    pool_in, gate_b, gate_c, conv_in = other
    pool_out = multiscale_pool(pool_in, pool_w, pool_scale)
    conv_out = gate_b * dwconv3(gate_c * conv_in, sconv_w)
    return jnp.concatenate([attn, pool_out, conv_out], axis=-1) @ w_o


def conv_ffn(h, w_up, w_conv, w_down):
    g, u = jnp.split(h @ w_up, 2, axis=-1)
    return (jax.nn.silu(dwconv3(g, w_conv)) * u) @ w_down


def setup_inputs(seed: int = 0) -> dict:
    key = jax.random.key(seed)
    ks = jax.random.split(key, 24)

    def nrm(k, shape, s):
        return jax.random.normal(k, shape, jnp.float32) * s

    def gain(k, shape):
        return 1.0 + 0.1 * jax.random.normal(k, shape, jnp.float32)

    return {
        'x': nrm(ks[0], (BATCH, SEQ, D_MODEL), 1.0),
        'c': nrm(ks[1], (BATCH, D_MODEL), 1.0),
        'ctx': nrm(ks[2], (BATCH, CTX_LEN, D_MODEL), 1.0),
        'c_ctx': nrm(ks[3], (D_MODEL,), 1.0),
        'ada_w': nrm(ks[4], (DEPTH, D_MODEL, 6 * D_MODEL), 0.5 * D_MODEL ** -0.5),
        'ada_b': nrm(ks[5], (DEPTH, 6 * D_MODEL), 0.02),
        'norm1_g': gain(ks[6], (DEPTH, D_MODEL)),
        'norm2_g': gain(ks[7], (DEPTH, D_MODEL)),
        'w_in': nrm(ks[8], (DEPTH, D_MODEL, IN_WIDTH), D_MODEL ** -0.5),
        'qc_g': gain(ks[9], (DEPTH, Q_RANK)),
        'kvc_g': gain(ks[10], (DEPTH, KV_RANK)),
        'w_q_up': nrm(ks[11], (DEPTH, Q_RANK, MLA_HEADS * QK_DIM), Q_RANK ** -0.5),
        'w_kv_up': nrm(ks[12], (DEPTH, KV_RANK, MLA_HEADS * (NOPE_DIM + V_DIM)), KV_RANK ** -0.5),
        'qn_g': gain(ks[13], (DEPTH, NOPE_DIM)),
        'qr_g': gain(ks[14], (DEPTH, ROPE_DIM)),
        'kn_g': gain(ks[15], (DEPTH, NOPE_DIM)),
        'kr_g': gain(ks[16], (DEPTH, ROPE_DIM)),
        'pool_w': nrm(ks[17], (DEPTH, POOL_GROUPS, POOL_CH, POOL_CH), POOL_CH ** -0.5),
        'pool_scale': gain(ks[18], (DEPTH, POOL_WIDTH)),
        'sconv_w': nrm(ks[19], (DEPTH, CONV_K, CONV_WIDTH), CONV_K ** -0.5),
        'w_o': nrm(ks[20], (DEPTH, MIX_WIDTH, D_MODEL), MIX_WIDTH ** -0.5),
        'ff_up': nrm(ks[21], (DEPTH, D_MODEL, 2 * D_FF), D_MODEL ** -0.5),
        'ff_conv': nrm(ks[22], (DEPTH, CONV_K, D_FF), CONV_K ** -0.5),
        'ff_down': nrm(ks[23], (DEPTH, D_FF, D_MODEL), D_FF ** -0.5),
    }


def reference(x, c, ctx, c_ctx, ada_w, ada_b, norm1_g, norm2_g, w_in, qc_g, kvc_g, w_q_up, w_kv_up,
              qn_g, qr_g, kn_g, kr_g, pool_w, pool_scale, sconv_w, w_o, ff_up, ff_conv, ff_down):
    n = x.shape[1]
    rope = axial_rope_tables(n)
    for i in range(DEPTH):
        feat_w = (w_in[i], qc_g[i], kvc_g[i], w_q_up[i], w_kv_up[i], qn_g[i], qr_g[i], kn_g[i], kr_g[i])
        out_w = (pool_w[i], pool_scale[i], sconv_w[i], w_o[i])
        ffn_w = (ff_up[i], ff_conv[i], ff_down[i])
        sh1x, sc1x, g1x, sh2x, sc2x, g2x = adaln(c, ada_w[i], ada_b[i])
        sh1c, sc1c, g1c, sh2c, sc2c, g2c = adaln(c_ctx, ada_w[i], ada_b[i])

        hx = rmsnorm(x, norm1_g[i]) * (1 + sc1x) + sh1x
        hc = rmsnorm(ctx, norm1_g[i]) * (1 + sc1c) + sh1c
        (qn_x, qr_x, kn_x, kr_x, v_x), rest_x = mixer_features(hx, *feat_w, rope)
        (qn_c, qr_c, kn_c, kr_c, v_c), rest_c = mixer_features(hc, *feat_w, None)

        k_nope = jnp.concatenate([kn_x, kn_c], axis=1)
        k_rope = jnp.concatenate([kr_x, kr_c], axis=1)
        v = jnp.concatenate([v_x, v_c], axis=1)
        attn_x = blocked_attention(qn_x, qr_x, k_nope, k_rope, v)
        x = x + g1x * mixer_output(attn_x, rest_x, *out_w)
        x = x + g2x * conv_ffn(rmsnorm(x, norm2_g[i]) * (1 + sc2x) + sh2x, *ffn_w)

        if i < DEPTH - 1:
            attn_c = attend(qn_c, qr_c, kn_c, kr_c, v_c)
            ctx = ctx + g1c * mixer_output(attn_c, rest_c, *out_w)
            ctx = ctx + g2c * conv_ffn(rmsnorm(ctx, norm2_g[i]) * (1 + sc2c) + sh2c, *ffn_w)
    return x
```

```python
import functools

import jax
import jax.numpy as jnp
import numpy as np
from jax import lax
from jax.experimental import pallas as pl
from jax.experimental.pallas import tpu as pltpu

F32 = jnp.float32
BF16 = jnp.bfloat16

GRID_W = 64
NOPE_DIM = 128
ROPE_DIM = 64
V_DIM = 128
ROPE_THETA = 10000.0
POOL_WINDOWS = (2, 4, 8, 16)
CONV_K = 3
EPS = 1e-6
N_MOD = 6

LANES = 128
SUBLANES = 8
MXU_DIM = 256
HEAD_PAD = MXU_DIM
HALO = SUBLANES
VMEM_LIMIT = 52 * 1024 * 1024

ROW_TILE = 512
Q_TILE = 256
FF_CHUNK = 256


def _rms(x, width=None):
    width = x.shape[-1] if width is None else width
    return x * lax.rsqrt(jnp.sum(x * x, axis=-1, keepdims=True) * (1.0 / width) + EPS)


def _sigmoid(x):
    return 1.0 / (1.0 + jnp.exp(-x))


def _dot(a, b):
    return jnp.dot(a, b, preferred_element_type=F32)


def _adaln_kernel(c_ref, w_ref, b_ref, o_ref):
    c = c_ref[...]
    s = (c * _sigmoid(c)).astype(BF16)
    o_ref[0] = _dot(s, w_ref[0].astype(BF16)) + b_ref[0]


def _adaln(cvec, ada_w, ada_b):
    depth, d, width = ada_w.shape
    rows = cvec.shape[0]
    tn = width // 4
    return pl.pallas_call(
        _adaln_kernel,
        out_shape=jax.ShapeDtypeStruct((depth, rows, width), F32),
        grid=(depth, width // tn),
        in_specs=[
            pl.BlockSpec((rows, d), lambda i, n: (0, 0)),
            pl.BlockSpec((1, d, tn), lambda i, n: (i, 0, n)),
            pl.BlockSpec((1, 1, tn), lambda i, n: (i, 0, n)),
        ],
        out_specs=pl.BlockSpec((1, rows, tn), lambda i, n: (i, 0, n)),
        compiler_params=pltpu.CompilerParams(
            dimension_semantics=("arbitrary", "arbitrary"), vmem_limit_bytes=VMEM_LIMIT),
        name="adaln",
    )(cvec, ada_w, ada_b.reshape(depth, 1, width))


def _pre_kernel(x_ref, mod_ref, ng_ref, w_in_ref, qcg_ref, kvcg_ref, wq_ref, wkv_ref,
                qng_ref, qrg_ref, kng_ref, krg_ref, cs_ref,
                qT_ref, k_ref, vT_ref, rest_ref, *, n_heads, q_rank, kv_rank, side, scale):
    x = x_ref[0]
    h = _rms(x) * ng_ref[...]
    h = h * (1.0 + mod_ref[0, 1:2, :]) + mod_ref[0, 0:1, :]
    hb = h.astype(BF16)

    def proj(lo, width):
        return _dot(hb, w_in_ref[:, lo:lo + width])

    o_kv = q_rank
    o_kr = o_kv + kv_rank
    o_pool = o_kr + 2 * ROPE_DIM
    o_gb = o_pool + side
    o_gc = o_gb + side
    o_cv = o_gc + side

    cs = cs_ref[...]
    lane = lax.broadcasted_iota(jnp.int32, (1, LANES), 1)
    keep = (lane < ROPE_DIM).astype(F32)

    def rope(tile, gain, out_scale):
        t = (_rms(tile) * gain) * cs
        return (t + pltpu.roll(t, ROPE_DIM, 1)) * (keep * out_scale)

    qc = (_rms(proj(0, q_rank)) * qcg_ref[...]).astype(BF16)
    for hd in range(n_heads):
        qh = _dot(qc, wq_ref[:, hd * HEAD_PAD:(hd + 1) * HEAD_PAD])
        qn = (_rms(qh[:, :NOPE_DIM]) * qng_ref[...]) * scale
        qr = rope(qh[:, NOPE_DIM:], qrg_ref[...], scale)
        qT_ref[0, hd] = jnp.concatenate([qn, qr], axis=1).T.astype(BF16)

    kvc = (_rms(proj(o_kv, kv_rank)) * kvcg_ref[...]).astype(BF16)
    kr = rope(proj(o_kr, 2 * ROPE_DIM), krg_ref[...], 1.0).astype(BF16)
    for hd in range(n_heads):
        kv = _dot(kvc, wkv_ref[:, hd * (NOPE_DIM + V_DIM):(hd + 1) * (NOPE_DIM + V_DIM)])
        k_ref[0, hd, :, :NOPE_DIM] = (_rms(kv[:, :NOPE_DIM]) * kng_ref[...]).astype(BF16)
        k_ref[0, hd, :, NOPE_DIM:] = kr
        vT_ref[0, hd, 0] = kv[:, NOPE_DIM:].T.astype(BF16)

    rest_ref[0, :, 0:side] = proj(o_pool, side)
    rest_ref[0, :, side:2 * side] = proj(o_gb, side)
    rest_ref[0, :, 2 * side:3 * side] = proj(o_gc, side) * proj(o_cv, side)


def _const_spec(shape):
    nd = len(shape)
    return pl.BlockSpec(shape, lambda *_: (0,) * nd, pipeline_mode=pl.Buffered(1))


def _pre(x, mod, lw, cs, *, tm):
    b, n, d = x.shape
    n_heads = lw["n_heads"]
    side = lw["side"]
    nt = n // tm
    mod_rows = mod.shape[0]
    mod_map = (lambda bi, j: (bi, 0, 0)) if mod_rows > 1 else (lambda bi, j: (0, 0, 0))
    kern = functools.partial(_pre_kernel, n_heads=n_heads, q_rank=lw["q_rank"],
                             kv_rank=lw["kv_rank"], side=side, scale=lw["scale"])
    consts = [lw["norm1_g"], lw["w_in"], lw["qc_g"], lw["kvc_g"], lw["w_q"], lw["w_kv"],
              lw["qn_g"], lw["qr_g"], lw["kn_g"], lw["kr_g"]]
    return pl.pallas_call(
        kern,
        out_shape=(
            jax.ShapeDtypeStruct((b, n_heads, HEAD_PAD, n), BF16),
            jax.ShapeDtypeStruct((b, n_heads, n, HEAD_PAD), BF16),
            jax.ShapeDtypeStruct((b, n_heads, nt, V_DIM, tm), BF16),
            jax.ShapeDtypeStruct((b, n, 3 * side), F32),
        ),
        grid=(b, nt),
        in_specs=[pl.BlockSpec((1, tm, d), lambda bi, j: (bi, j, 0)),
                  pl.BlockSpec((1, N_MOD, d), mod_map)]
                 + [_const_spec(c.shape) for c in consts]
                 + [pl.BlockSpec((tm, LANES), lambda bi, j: (j, 0))],
        out_specs=(
            pl.BlockSpec((1, n_heads, HEAD_PAD, tm), lambda bi, j: (bi, 0, 0, j)),
            pl.BlockSpec((1, n_heads, tm, HEAD_PAD), lambda bi, j: (bi, 0, j, 0)),
            pl.BlockSpec((1, n_heads, 1, V_DIM, tm), lambda bi, j: (bi, 0, j, 0, 0)),
            pl.BlockSpec((1, tm, 3 * side), lambda bi, j: (bi, j, 0)),
        ),
        compiler_params=pltpu.CompilerParams(
            dimension_semantics=("arbitrary", "arbitrary"), vmem_limit_bytes=VMEM_LIMIT),
        name="pre_proj",
    )(x, mod, *consts, cs)


def _attn_kernel(qT_ref, *refs, n_chunks):
    o_ref = refs[-1]
    qT = qT_ref[0, 0]
    tq = qT.shape[1]

    def step(k_ref, vT_ref, c, carry):
        m, l, acc = carry
        s = _dot(k_ref[0, 0, c], qT)
        m_new = jnp.maximum(m, jnp.max(s, axis=0, keepdims=True))
        alpha = jnp.exp(m - m_new)
        p = jnp.exp(s - m_new)
        l = alpha * l + jnp.sum(p, axis=0, keepdims=True)
        acc = alpha * acc + _dot(vT_ref[0, 0, c], p.astype(BF16))
        return m_new, l, acc

    carry = (jnp.full((1, tq), -jnp.inf, F32), jnp.zeros((1, tq), F32),
             jnp.zeros((V_DIM, tq), F32))
    for src, nc in enumerate(n_chunks):
        k_ref, vT_ref = refs[2 * src], refs[2 * src + 1]
        if nc == 1:
            carry = step(k_ref, vT_ref, 0, carry)
        else:
            carry = lax.fori_loop(0, nc, functools.partial(step, k_ref, vT_ref), carry)
    _, l, acc = carry
    o_ref[0] = (acc / l).T.astype(BF16)


def _attn(qT, sources, *, tq):
    b, n_heads, _, n = qT.shape
    in_specs = [pl.BlockSpec((1, 1, HEAD_PAD, tq), lambda bi, hi, i: (bi, hi, 0, i))]
    args = [qT]
    n_chunks = []
    for k, vT in sources:
        nc, tk = k.shape[2], k.shape[3]
        n_chunks.append(nc)
        in_specs.append(pl.BlockSpec((1, 1, nc, tk, HEAD_PAD), lambda bi, hi, i: (bi, hi, 0, 0, 0)))
        in_specs.append(pl.BlockSpec((1, 1, nc, V_DIM, tk), lambda bi, hi, i: (bi, hi, 0, 0, 0)))
        args += [k, vT]
    return pl.pallas_call(
        functools.partial(_attn_kernel, n_chunks=tuple(n_chunks)),
        out_shape=jax.ShapeDtypeStruct((b, n, n_heads * V_DIM), BF16),
        grid=(b, n_heads, n // tq),
        in_specs=in_specs,
        out_specs=pl.BlockSpec((1, tq, V_DIM), lambda bi, hi, i: (bi, i, hi)),
        compiler_params=pltpu.CompilerParams(
            dimension_semantics=("arbitrary", "arbitrary", "arbitrary"),
            vmem_limit_bytes=VMEM_LIMIT),
        name="flash_attn",
    )(*args)


def _halo_specs(tm, width, nt):
    per = tm // HALO

    def prev_map(bi, j):
        return (bi, jnp.maximum(j * per - 1, 0), 0)

    def next_map(bi, j):
        return (bi, jnp.minimum((j + 1) * per, nt * per - 1), 0)

    return [pl.BlockSpec((1, tm, width), lambda bi, j: (bi, j, 0)),
            pl.BlockSpec((1, HALO, width), prev_map),
            pl.BlockSpec((1, HALO, width), next_map)]


def _shift_rows(a, s):
    return pltpu.roll(a, s % a.shape[0], 0)


def _mix_kernel(x_ref, attn_ref, rest_ref, rprev_ref, rnext_ref, mod_ref, pool_w_ref,
                pool_scale_ref, sconv_ref, wo_ref, o_ref, *, side, attn_width, nt):
    j = pl.program_id(1)
    tm = x_ref.shape[1]
    seq_len = tm * nt
    rest = rest_ref[0]
    prev = jnp.where(j == 0, 0.0, rprev_ref[0])
    nxt = jnp.where(j == nt - 1, 0.0, rnext_ref[0])
    ext = jnp.concatenate([prev, rest, nxt], axis=0)
    main = slice(HALO, HALO + tm)

    pe = ext[:, 0:side]
    w2 = pe + _shift_rows(pe, 1)
    w4 = _shift_rows(w2, 1) + _shift_rows(w2, -1)
    w8 = _shift_rows(w4, 2) + _shift_rows(w4, -2)
    w16 = _shift_rows(w8, 4) + _shift_rows(w8, -4)
    n_groups = len(POOL_WINDOWS)
    grp = lax.broadcasted_iota(jnp.int32, (1, side), 1) // (side // n_groups)
    total = jnp.where(grp == 0, w2[main], jnp.where(grp == 1, w4[main],
                      jnp.where(grp == 2, w8[main], w16[main])))
    half = jnp.where(grp == 0, POOL_WINDOWS[0] // 2, jnp.where(grp == 1, POOL_WINDOWS[1] // 2,
                     jnp.where(grp == 2, POOL_WINDOWS[2] // 2, POOL_WINDOWS[3] // 2)))
    t = j * tm + lax.broadcasted_iota(jnp.int32, (tm, 1), 0)
    count = jnp.minimum(t + half, seq_len) - jnp.maximum(t - half, 0)
    diff = total / count.astype(F32) - rest[:, 0:side]
    pool_out = _dot(diff.astype(BF16), pool_w_ref[...]) * pool_scale_ref[...]

    ue = ext[:, 2 * side:3 * side]
    conv = (_shift_rows(ue, 1)[main] * sconv_ref[0:1, :] + rest[:, 2 * side:3 * side] * sconv_ref[1:2, :]
            + _shift_rows(ue, -1)[main] * sconv_ref[2:3, :])
    conv_out = rest[:, side:2 * side] * conv

    mix = (_dot(attn_ref[0], wo_ref[0:attn_width, :])
           + _dot(pool_out.astype(BF16), wo_ref[attn_width:attn_width + side, :])
           + _dot(conv_out.astype(BF16), wo_ref[attn_width + side:, :]))
    o_ref[0] = x_ref[0] + mod_ref[0, 2:3, :] * mix


def _mix(x, attn, rest, mod, lw, *, tm):
    b, n, d = x.shape
    side = lw["side"]
    attn_width = attn.shape[-1]
    mod_rows = mod.shape[0]
    mod_map = (lambda bi, j: (bi, 0, 0)) if mod_rows > 1 else (lambda bi, j: (0, 0, 0))
    consts = [lw["pool_w"], lw["pool_scale"], lw["sconv_w"], lw["w_o"]]
    nt = n // tm
    return pl.pallas_call(
        functools.partial(_mix_kernel, side=side, attn_width=attn_width, nt=nt),
        out_shape=jax.ShapeDtypeStruct((b, n, d), F32),
        grid=(b, nt),
        in_specs=[pl.BlockSpec((1, tm, d), lambda bi, j: (bi, j, 0)),
                  pl.BlockSpec((1, tm, attn_width), lambda bi, j: (bi, j, 0))]
                 + _halo_specs(tm, 3 * side, nt)
                 + [pl.BlockSpec((1, N_MOD, d), mod_map)]
                 + [_const_spec(c.shape) for c in consts],
        out_specs=pl.BlockSpec((1, tm, d), lambda bi, j: (bi, j, 0)),
        compiler_params=pltpu.CompilerParams(
            dimension_semantics=("arbitrary", "arbitrary"), vmem_limit_bytes=VMEM_LIMIT),
        name="mixer_out",
    )(x, attn, rest, rest, rest, mod, *consts)


def _ffn_kernel(x_ref, xprev_ref, xnext_ref, mod_ref, ng_ref, wg_ref, wu_ref, wconv_ref, wd_ref,
                o_ref, acc_ref, *, nt):
    j = pl.program_id(1)
    tm = x_ref.shape[1]
    seq_len = tm * nt
    x = x_ref[0]
    xe = jnp.concatenate([xprev_ref[0], x, xnext_ref[0]], axis=0)
    he = _rms(xe) * ng_ref[...]
    he = he * (1.0 + mod_ref[0, 4:5, :]) + mod_ref[0, 3:4, :]
    main = slice(HALO, HALO + tm)
    hm = he[main].astype(BF16)
    he = he.astype(BF16)
    pos = j * tm - HALO + lax.broadcasted_iota(jnp.int32, (tm + 2 * HALO, 1), 0)
    inside = jnp.logical_and(pos >= 0, pos < seq_len)

    acc_ref[...] = jnp.zeros_like(acc_ref)

    def chunk(c, carry):
        g = jnp.where(inside, _dot(he, wg_ref[c]), 0.0)
        u = _dot(hm, wu_ref[c])
        w = wconv_ref[c]
        cv = (_shift_rows(g, 1)[main] * w[0:1, :] + g[main] * w[1:2, :]
              + _shift_rows(g, -1)[main] * w[2:3, :])
        act = (cv * _sigmoid(cv)) * u
        acc_ref[...] += _dot(act.astype(BF16), wd_ref[c])
        return carry

    lax.fori_loop(0, wg_ref.shape[0], chunk, 0)
    o_ref[0] = x + mod_ref[0, 5:6, :] * acc_ref[...]


def _ffn(x, mod, lw, *, tm):
    b, n, d = x.shape
    mod_rows = mod.shape[0]
    mod_map = (lambda bi, j: (bi, 0, 0)) if mod_rows > 1 else (lambda bi, j: (0, 0, 0))
    consts = [lw["norm2_g"], lw["ff_g"], lw["ff_u"], lw["ff_conv"], lw["ff_down"]]
    nt = n // tm
    return pl.pallas_call(
        functools.partial(_ffn_kernel, nt=nt),
        out_shape=jax.ShapeDtypeStruct((b, n, d), F32),
        grid=(b, nt),
        in_specs=_halo_specs(tm, d, nt)
                 + [pl.BlockSpec((1, N_MOD, d), mod_map)]
                 + [_const_spec(c.shape) for c in consts],
        out_specs=pl.BlockSpec((1, tm, d), lambda bi, j: (bi, j, 0)),
        scratch_shapes=[pltpu.VMEM((tm, d), F32)],
        compiler_params=pltpu.CompilerParams(
            dimension_semantics=("arbitrary", "arbitrary"), vmem_limit_bytes=VMEM_LIMIT),
        name="conv_ffn",
    )(x, x, x, mod, *consts)


def _half_swap(n):
    quarter = ROPE_DIM // 4
    idx = np.arange(n).reshape(-1, 2, quarter)
    return idx[:, ::-1, :].reshape(-1)


def _rot_columns(w):
    quarter = ROPE_DIM // 4
    sign = np.tile(np.concatenate([-np.ones(quarter), np.ones(quarter)]), ROPE_DIM // (2 * quarter))
    return w[:, _half_swap(ROPE_DIM)] * jnp.asarray(sign, w.dtype)


def _layer_weights(i, p):
    d = p["w_in"].shape[1]
    q_rank = p["qc_g"].shape[1]
    kv_rank = p["kvc_g"].shape[1]
    side = p["pool_scale"].shape[1]
    qk_dim = NOPE_DIM + ROPE_DIM
    n_heads = p["w_q_up"].shape[2] // qk_dim
    w_in = p["w_in"][i]
    o_kr = q_rank + kv_rank
    w_in_ext = jnp.concatenate(
        [w_in[:, :o_kr + ROPE_DIM], _rot_columns(w_in[:, o_kr:o_kr + ROPE_DIM]), w_in[:, o_kr + ROPE_DIM:]],
        axis=1)
    wq = p["w_q_up"][i].reshape(q_rank, n_heads, qk_dim)
    wq_ext = jnp.concatenate(
        [wq, jax.vmap(_rot_columns, in_axes=1, out_axes=1)(wq[:, :, NOPE_DIM:])], axis=2)
    swap = _half_swap(ROPE_DIM)
    d_ff = p["ff_conv"].shape[2]
    nc = d_ff // FF_CHUNK
    ff_up = p["ff_up"][i]

    def chunked_cols(w):
        return w.reshape(w.shape[0], nc, FF_CHUNK).transpose(1, 0, 2)

    pool_w = p["pool_w"][i]
    n_groups, pool_ch = pool_w.shape[0], pool_w.shape[1]
    pool_bd = jnp.zeros((side, side), F32)
    for g in range(n_groups):
        pool_bd = pool_bd.at[g * pool_ch:(g + 1) * pool_ch, g * pool_ch:(g + 1) * pool_ch].set(pool_w[g])

    def row(v):
        return v.reshape(1, -1)

    return dict(
        n_heads=n_heads, q_rank=q_rank, kv_rank=kv_rank, side=side, scale=float(qk_dim) ** -0.5,
        norm1_g=row(p["norm1_g"][i]), norm2_g=row(p["norm2_g"][i]),
        w_in=w_in_ext.astype(BF16), qc_g=row(p["qc_g"][i]), kvc_g=row(p["kvc_g"][i]),
        w_q=wq_ext.reshape(q_rank, n_heads * HEAD_PAD).astype(BF16),
        w_kv=p["w_kv_up"][i].astype(BF16),
        qn_g=row(p["qn_g"][i]), kn_g=row(p["kn_g"][i]),
        qr_g=row(jnp.concatenate([p["qr_g"][i], p["qr_g"][i][swap]])),
        kr_g=row(jnp.concatenate([p["kr_g"][i], p["kr_g"][i][swap]])),
        pool_w=pool_bd.astype(BF16), pool_scale=row(p["pool_scale"][i]), sconv_w=p["sconv_w"][i],
        w_o=p["w_o"][i].astype(BF16),
        ff_g=chunked_cols(ff_up[:, :d_ff]).astype(BF16), ff_u=chunked_cols(ff_up[:, d_ff:]).astype(BF16),
        ff_conv=p["ff_conv"][i].reshape(CONV_K, nc, FF_CHUNK).transpose(1, 0, 2),
        ff_down=p["ff_down"][i].reshape(nc, FF_CHUNK, d).astype(BF16),
    )


def _rope_table(n):
    rows = n // GRID_W
    row = jnp.repeat(jnp.arange(rows), GRID_W).astype(F32)
    col = jnp.tile(jnp.arange(GRID_W), rows).astype(F32)
    n_freq = ROPE_DIM // 4
    inv = ROPE_THETA ** (-jnp.arange(n_freq, dtype=F32) / n_freq)
    ar = row[:, None] * inv
    ac = col[:, None] * inv
    ang = jnp.concatenate([ar, ar, ac, ac], axis=-1)
    return jnp.concatenate([jnp.cos(ang), jnp.sin(ang)], axis=-1)


def kernel(x, c, ctx, c_ctx, ada_w, ada_b, norm1_g, norm2_g, w_in, qc_g, kvc_g, w_q_up, w_kv_up,
           qn_g, qr_g, kn_g, kr_g, pool_w, pool_scale, sconv_w, w_o, ff_up, ff_conv, ff_down):
    params = dict(norm1_g=norm1_g, norm2_g=norm2_g, w_in=w_in, qc_g=qc_g, kvc_g=kvc_g,
                  w_q_up=w_q_up, w_kv_up=w_kv_up, qn_g=qn_g, qr_g=qr_g, kn_g=kn_g, kr_g=kr_g,
                  pool_w=pool_w, pool_scale=pool_scale, sconv_w=sconv_w, w_o=w_o, ff_up=ff_up,
                  ff_conv=ff_conv, ff_down=ff_down)
    depth = ada_w.shape[0]
    b, n, d = x.shape
    n_ctx = ctx.shape[1]
    tm = min(ROW_TILE, n)
    tq = min(Q_TILE, n)

    mod_rows = -(-(b + 1) // SUBLANES) * SUBLANES
    cvec = jnp.zeros((mod_rows, d), F32).at[:b].set(c).at[b].set(c_ctx)
    mods = _adaln(cvec, ada_w, ada_b).reshape(depth, mod_rows, N_MOD, d)

    cs_x = _rope_table(n)
    cs_c = jnp.concatenate([jnp.ones((n_ctx, ROPE_DIM), F32), jnp.zeros((n_ctx, ROPE_DIM), F32)], axis=-1)

    for i in range(depth):
        lw = _layer_weights(i, params)
        mod_x = mods[i, :b]
        mod_c = mods[i, b:b + 1]
        qT_x, k_x, vT_x, rest_x = _pre(x, mod_x, lw, cs_x, tm=tm)
        qT_c, k_c, vT_c, rest_c = _pre(ctx, mod_c, lw, cs_c, tm=n_ctx)
        nh = lw["n_heads"]
        src_x = (k_x.reshape(b, nh, n // tm, tm, HEAD_PAD), vT_x)
        src_c = (k_c.reshape(b, nh, 1, n_ctx, HEAD_PAD), vT_c)
        attn_x = _attn(qT_x, [src_x, src_c], tq=tq)
        x = _mix(x, attn_x, rest_x, mod_x, lw, tm=tm)
        x = _ffn(x, mod_x, lw, tm=tm)
        if i < depth - 1:
            attn_c = _attn(qT_c, [src_c], tq=min(tq, n_ctx))
            ctx = _mix(ctx, attn_c, rest_c, mod_c, lw, tm=n_ctx)
            ctx = _ffn(ctx, mod_c, lw, tm=n_ctx)
    return x
```

```python
import functools

import jax
import jax.numpy as jnp
import numpy as np
from jax import lax
from jax.experimental import pallas as pl
from jax.experimental.pallas import tpu as pltpu

F32 = jnp.float32
BF16 = jnp.bfloat16

GRID_W = 64
NOPE_DIM = 128
ROPE_DIM = 64
V_DIM = 128
ROPE_THETA = 10000.0
POOL_WINDOWS = (2, 4, 8, 16)
CONV_K = 3
EPS = 1e-6
N_MOD = 6

LANES = 128
SUBLANES = 8
MXU_DIM = 256
HEAD_PAD = MXU_DIM
HALO = SUBLANES
VMEM_LIMIT = 52 * 1024 * 1024

ROW_TILE = 512
Q_TILE = 256
FF_CHUNK = 256
ATTN_LOOKAHEAD = 3


def _rms(x, width=None):
    width = x.shape[-1] if width is None else width
    return x * lax.rsqrt(jnp.sum(x * x, axis=-1, keepdims=True) * (1.0 / width) + EPS)


def _sigmoid(x):
    return 1.0 / (1.0 + jnp.exp(-x))


def _dot(a, b):
    return jnp.dot(a, b, preferred_element_type=F32)


def _adaln_kernel(c_ref, w_ref, b_ref, o_ref):
    c = c_ref[...]
    s = (c * _sigmoid(c)).astype(BF16)
    o_ref[0] = _dot(s, w_ref[0].astype(BF16)) + b_ref[0]


def _adaln(cvec, ada_w, ada_b):
    depth, d, width = ada_w.shape
    rows = cvec.shape[0]
    tn = width // 4
    return pl.pallas_call(
        _adaln_kernel,
        out_shape=jax.ShapeDtypeStruct((depth, rows, width), F32),
        grid=(depth, width // tn),
        in_specs=[
            pl.BlockSpec((rows, d), lambda i, n: (0, 0)),
            pl.BlockSpec((1, d, tn), lambda i, n: (i, 0, n)),
            pl.BlockSpec((1, 1, tn), lambda i, n: (i, 0, n)),
        ],
        out_specs=pl.BlockSpec((1, rows, tn), lambda i, n: (i, 0, n)),
        compiler_params=pltpu.CompilerParams(
            dimension_semantics=("arbitrary", "arbitrary"), vmem_limit_bytes=VMEM_LIMIT),
        name="adaln",
    )(cvec, ada_w, ada_b.reshape(depth, 1, width))


def _pre_kernel(x_ref, mod_ref, ng_ref, w_in_ref, qcg_ref, kvcg_ref, wq_ref, wkv_ref,
                qng_ref, qrg_ref, kng_ref, krg_ref, cs_ref,
                qT_ref, k_ref, vT_ref, rest_ref, *, n_heads, q_rank, kv_rank, side, scale):
    x = x_ref[0]
    h = _rms(x) * ng_ref[...]
    h = h * (1.0 + mod_ref[0, 1:2, :]) + mod_ref[0, 0:1, :]
    hb = h.astype(BF16)

    def proj(lo, width):
        return _dot(hb, w_in_ref[:, lo:lo + width])

    o_kv = q_rank
    o_kr = o_kv + kv_rank
    o_pool = o_kr + 2 * ROPE_DIM
    o_gb = o_pool + side
    o_gc = o_gb + side
    o_cv = o_gc + side

    cs = cs_ref[...]
    lane = lax.broadcasted_iota(jnp.int32, (1, LANES), 1)
    keep = (lane < ROPE_DIM).astype(F32)

    def rope(tile, gain, out_scale):
        t = (_rms(tile) * gain) * cs
        return (t + pltpu.roll(t, ROPE_DIM, 1)) * (keep * out_scale)

    qc = (_rms(proj(0, q_rank)) * qcg_ref[...]).astype(BF16)
    for hd in range(n_heads):
        qh = _dot(qc, wq_ref[:, hd * HEAD_PAD:(hd + 1) * HEAD_PAD])
        qn = (_rms(qh[:, :NOPE_DIM]) * qng_ref[...]) * scale
        qr = rope(qh[:, NOPE_DIM:], qrg_ref[...], scale)
        qT_ref[0, hd] = jnp.concatenate([qn, qr], axis=1).T.astype(BF16)

    kvc = (_rms(proj(o_kv, kv_rank)) * kvcg_ref[...]).astype(BF16)
    kr = rope(proj(o_kr, 2 * ROPE_DIM), krg_ref[...], 1.0).astype(BF16)
    for hd in range(n_heads):
        kv = _dot(kvc, wkv_ref[:, hd * (NOPE_DIM + V_DIM):(hd + 1) * (NOPE_DIM + V_DIM)])
        k_ref[0, hd, :, :NOPE_DIM] = (_rms(kv[:, :NOPE_DIM]) * kng_ref[...]).astype(BF16)
        k_ref[0, hd, :, NOPE_DIM:] = kr
        vT_ref[0, hd, 0] = kv[:, NOPE_DIM:].T.astype(BF16)

    rest_ref[0, :, 0:side] = proj(o_pool, side)
    rest_ref[0, :, side:2 * side] = proj(o_gb, side)
    rest_ref[0, :, 2 * side:3 * side] = proj(o_gc, side) * proj(o_cv, side)


def _const_spec(shape):
    nd = len(shape)
    return pl.BlockSpec(shape, lambda *_: (0,) * nd, pipeline_mode=pl.Buffered(1))


def _pre(x, mod, lw, cs, *, tm):
    b, n, d = x.shape
    n_heads = lw["n_heads"]
    side = lw["side"]
    nt = n // tm
    mod_rows = mod.shape[0]
    mod_map = (lambda bi, j: (bi, 0, 0)) if mod_rows > 1 else (lambda bi, j: (0, 0, 0))
    kern = functools.partial(_pre_kernel, n_heads=n_heads, q_rank=lw["q_rank"],
                             kv_rank=lw["kv_rank"], side=side, scale=lw["scale"])
    consts = [lw["norm1_g"], lw["w_in"], lw["qc_g"], lw["kvc_g"], lw["w_q"], lw["w_kv"],
              lw["qn_g"], lw["qr_g"], lw["kn_g"], lw["kr_g"]]
    return pl.pallas_call(
        kern,
        out_shape=(
            jax.ShapeDtypeStruct((b, n_heads, HEAD_PAD, n), BF16),
            jax.ShapeDtypeStruct((b, n_heads, n, HEAD_PAD), BF16),
            jax.ShapeDtypeStruct((b, n_heads, nt, V_DIM, tm), BF16),
            jax.ShapeDtypeStruct((b, n, 3 * side), F32),
        ),
        grid=(b, nt),
        in_specs=[pl.BlockSpec((1, tm, d), lambda bi, j: (bi, j, 0)),
                  pl.BlockSpec((1, N_MOD, d), mod_map)]
                 + [_const_spec(c.shape) for c in consts]
                 + [pl.BlockSpec((tm, LANES), lambda bi, j: (j, 0))],
        out_specs=(
            pl.BlockSpec((1, n_heads, HEAD_PAD, tm), lambda bi, j: (bi, 0, 0, j)),
            pl.BlockSpec((1, n_heads, tm, HEAD_PAD), lambda bi, j: (bi, 0, j, 0)),
            pl.BlockSpec((1, n_heads, 1, V_DIM, tm), lambda bi, j: (bi, 0, j, 0, 0)),
            pl.BlockSpec((1, tm, 3 * side), lambda bi, j: (bi, j, 0)),
        ),
        compiler_params=pltpu.CompilerParams(
            dimension_semantics=("arbitrary", "arbitrary"), vmem_limit_bytes=VMEM_LIMIT),
        name="pre_proj",
    )(x, mod, *consts, cs)


def _col_reduce(x, op):
    rows = x.shape[0]
    while rows % (2 * SUBLANES) == 0:
        rows //= 2
        x = op(x[:rows], x[rows:])
    if op is jnp.add:
        return jnp.sum(x, axis=0, keepdims=True)
    return jnp.max(x, axis=0, keepdims=True)


def _attn_kernel(qT_ref, *refs, n_chunks):
    o_ref = refs[-1]
    qT = qT_ref[0, 0]
    chunks = [(refs[2 * src], refs[2 * src + 1], c)
              for src, nc in enumerate(n_chunks) for c in range(nc)]

    def scores(i):
        k_ref, _, c = chunks[i]
        return _dot(k_ref[0, 0, c], qT)

    n = len(chunks)
    ahead = min(ATTN_LOOKAHEAD, n)
    s_q, m_q, a_q = [], [], []
    m_run = None

    def issue(i):
        nonlocal m_run
        s = scores(i)
        cm = _col_reduce(s, jnp.maximum)
        m_new = cm if m_run is None else jnp.maximum(m_run, cm)
        a_q.append(None if m_run is None else jnp.exp2(m_run - m_new))
        s_q.append(s)
        m_q.append(m_new)
        m_run = m_new

    for i in range(ahead):
        issue(i)
    l = acc = None
    for i, (_, vT_ref, c) in enumerate(chunks):
        if i + ahead < n:
            issue(i + ahead)
        s, m, alpha = s_q.pop(0), m_q.pop(0), a_q.pop(0)
        p = jnp.exp2(s - m)
        p_sum = _col_reduce(p, jnp.add)
        pv = _dot(vT_ref[0, 0, c], p.astype(BF16))
        if alpha is None:
            l, acc = p_sum, pv
        else:
            l, acc = alpha * l + p_sum, alpha * acc + pv
    o_ref[0] = (acc * (1.0 / l)).T.astype(BF16)


def _attn(qT, sources, *, tq):
    b, n_heads, _, n = qT.shape
    in_specs = [pl.BlockSpec((1, 1, HEAD_PAD, tq), lambda bi, hi, i: (bi, hi, 0, i))]
    args = [qT]
    n_chunks = []
    for k, vT in sources:
        nc, tk = k.shape[2], k.shape[3]
        n_chunks.append(nc)
        in_specs.append(pl.BlockSpec((1, 1, nc, tk, HEAD_PAD), lambda bi, hi, i: (bi, hi, 0, 0, 0)))
        in_specs.append(pl.BlockSpec((1, 1, nc, V_DIM, tk), lambda bi, hi, i: (bi, hi, 0, 0, 0)))
        args += [k, vT]
    return pl.pallas_call(
        functools.partial(_attn_kernel, n_chunks=tuple(n_chunks)),
        out_shape=jax.ShapeDtypeStruct((b, n, n_heads * V_DIM), BF16),
        grid=(b, n_heads, n // tq),
        in_specs=in_specs,
        out_specs=pl.BlockSpec((1, tq, V_DIM), lambda bi, hi, i: (bi, i, hi)),
        compiler_params=pltpu.CompilerParams(
            dimension_semantics=("arbitrary", "arbitrary", "arbitrary"),
            vmem_limit_bytes=VMEM_LIMIT),
        name="flash_attn",
    )(*args)


def _halo_specs(tm, width, nt):
    per = tm // HALO

    def prev_map(bi, j):
        return (bi, jnp.maximum(j * per - 1, 0), 0)

    def next_map(bi, j):
        return (bi, jnp.minimum((j + 1) * per, nt * per - 1), 0)

    return [pl.BlockSpec((1, tm, width), lambda bi, j: (bi, j, 0)),
            pl.BlockSpec((1, HALO, width), prev_map),
            pl.BlockSpec((1, HALO, width), next_map)]


def _shift_rows(a, s):
    return pltpu.roll(a, s % a.shape[0], 0)


def _mix_kernel(x_ref, attn_ref, rest_ref, rprev_ref, rnext_ref, mod_ref, pool_w_ref,
                pool_scale_ref, sconv_ref, wo_ref, o_ref, *, side, attn_width, nt):
    j = pl.program_id(1)
    tm = x_ref.shape[1]
    seq_len = tm * nt
    rest = rest_ref[0]
    prev = jnp.where(j == 0, 0.0, rprev_ref[0])
    nxt = jnp.where(j == nt - 1, 0.0, rnext_ref[0])
    ext = jnp.concatenate([prev, rest, nxt], axis=0)
    main = slice(HALO, HALO + tm)

    pe = ext[:, 0:side]
    w2 = pe + _shift_rows(pe, 1)
    w4 = _shift_rows(w2, 1) + _shift_rows(w2, -1)
    w8 = _shift_rows(w4, 2) + _shift_rows(w4, -2)
    w16 = _shift_rows(w8, 4) + _shift_rows(w8, -4)
    n_groups = len(POOL_WINDOWS)
    grp = lax.broadcasted_iota(jnp.int32, (1, side), 1) // (side // n_groups)
    total = jnp.where(grp == 0, w2[main], jnp.where(grp == 1, w4[main],
                      jnp.where(grp == 2, w8[main], w16[main])))
    half = jnp.where(grp == 0, POOL_WINDOWS[0] // 2, jnp.where(grp == 1, POOL_WINDOWS[1] // 2,
                     jnp.where(grp == 2, POOL_WINDOWS[2] // 2, POOL_WINDOWS[3] // 2)))
    t = j * tm + lax.broadcasted_iota(jnp.int32, (tm, 1), 0)
    count = jnp.minimum(t + half, seq_len) - jnp.maximum(t - half, 0)
    diff = total / count.astype(F32) - rest[:, 0:side]
    pool_out = _dot(diff.astype(BF16), pool_w_ref[...]) * pool_scale_ref[...]

    ue = ext[:, 2 * side:3 * side]
    conv = (_shift_rows(ue, 1)[main] * sconv_ref[0:1, :] + rest[:, 2 * side:3 * side] * sconv_ref[1:2, :]
            + _shift_rows(ue, -1)[main] * sconv_ref[2:3, :])
    conv_out = rest[:, side:2 * side] * conv

    mix = (_dot(attn_ref[0], wo_ref[0:attn_width, :])
           + _dot(pool_out.astype(BF16), wo_ref[attn_width:attn_width + side, :])
           + _dot(conv_out.astype(BF16), wo_ref[attn_width + side:, :]))
    o_ref[0] = x_ref[0] + mod_ref[0, 2:3, :] * mix


def _mix(x, attn, rest, mod, lw, *, tm):
    b, n, d = x.shape
    side = lw["side"]
    attn_width = attn.shape[-1]
    mod_rows = mod.shape[0]
    mod_map = (lambda bi, j: (bi, 0, 0)) if mod_rows > 1 else (lambda bi, j: (0, 0, 0))
    consts = [lw["pool_w"], lw["pool_scale"], lw["sconv_w"], lw["w_o"]]
    nt = n // tm
    return pl.pallas_call(
        functools.partial(_mix_kernel, side=side, attn_width=attn_width, nt=nt),
        out_shape=jax.ShapeDtypeStruct((b, n, d), F32),
        grid=(b, nt),
        in_specs=[pl.BlockSpec((1, tm, d), lambda bi, j: (bi, j, 0)),
                  pl.BlockSpec((1, tm, attn_width), lambda bi, j: (bi, j, 0))]
                 + _halo_specs(tm, 3 * side, nt)
                 + [pl.BlockSpec((1, N_MOD, d), mod_map)]
                 + [_const_spec(c.shape) for c in consts],
        out_specs=pl.BlockSpec((1, tm, d), lambda bi, j: (bi, j, 0)),
        compiler_params=pltpu.CompilerParams(
            dimension_semantics=("arbitrary", "arbitrary"), vmem_limit_bytes=VMEM_LIMIT),
        name="mixer_out",
    )(x, attn, rest, rest, rest, mod, *consts)


def _ffn_kernel(x_ref, xprev_ref, xnext_ref, mod_ref, ng_ref, wg_ref, wu_ref, wconv_ref, wd_ref,
                o_ref, acc_ref, *, nt):
    j = pl.program_id(1)
    tm = x_ref.shape[1]
    seq_len = tm * nt
    x = x_ref[0]
    xe = jnp.concatenate([xprev_ref[0], x, xnext_ref[0]], axis=0)
    he = _rms(xe) * ng_ref[...]
    he = he * (1.0 + mod_ref[0, 4:5, :]) + mod_ref[0, 3:4, :]
    main = slice(HALO, HALO + tm)
    hm = he[main].astype(BF16)
    he = he.astype(BF16)
    pos = j * tm - HALO + lax.broadcasted_iota(jnp.int32, (tm + 2 * HALO, 1), 0)
    inside = jnp.logical_and(pos >= 0, pos < seq_len)

    acc_ref[...] = jnp.zeros_like(acc_ref)

    def chunk(c, carry):
        g = jnp.where(inside, _dot(he, wg_ref[c]), 0.0)
        u = _dot(hm, wu_ref[c])
        w = wconv_ref[c]
        cv = (_shift_rows(g, 1)[main] * w[0:1, :] + g[main] * w[1:2, :]
              + _shift_rows(g, -1)[main] * w[2:3, :])
        act = (cv * _sigmoid(cv)) * u
        acc_ref[...] += _dot(act.astype(BF16), wd_ref[c])
        return carry

    lax.fori_loop(0, wg_ref.shape[0], chunk, 0)
    o_ref[0] = x + mod_ref[0, 5:6, :] * acc_ref[...]


def _ffn(x, mod, lw, *, tm):
    b, n, d = x.shape
    mod_rows = mod.shape[0]
    mod_map = (lambda bi, j: (bi, 0, 0)) if mod_rows > 1 else (lambda bi, j: (0, 0, 0))
    consts = [lw["norm2_g"], lw["ff_g"], lw["ff_u"], lw["ff_conv"], lw["ff_down"]]
    nt = n // tm
    return pl.pallas_call(
        functools.partial(_ffn_kernel, nt=nt),
        out_shape=jax.ShapeDtypeStruct((b, n, d), F32),
        grid=(b, nt),
        in_specs=_halo_specs(tm, d, nt)
                 + [pl.BlockSpec((1, N_MOD, d), mod_map)]
                 + [_const_spec(c.shape) for c in consts],
        out_specs=pl.BlockSpec((1, tm, d), lambda bi, j: (bi, j, 0)),
        scratch_shapes=[pltpu.VMEM((tm, d), F32)],
        compiler_params=pltpu.CompilerParams(
            dimension_semantics=("arbitrary", "arbitrary"), vmem_limit_bytes=VMEM_LIMIT),
        name="conv_ffn",
    )(x, x, x, mod, *consts)


def _half_swap(n):
    quarter = ROPE_DIM // 4
    idx = np.arange(n).reshape(-1, 2, quarter)
    return idx[:, ::-1, :].reshape(-1)


def _rot_columns(w):
    quarter = ROPE_DIM // 4
    sign = np.tile(np.concatenate([-np.ones(quarter), np.ones(quarter)]), ROPE_DIM // (2 * quarter))
    return w[:, _half_swap(ROPE_DIM)] * jnp.asarray(sign, w.dtype)


def _layer_weights(i, p):
    d = p["w_in"].shape[1]
    q_rank = p["qc_g"].shape[1]
    kv_rank = p["kvc_g"].shape[1]
    side = p["pool_scale"].shape[1]
    qk_dim = NOPE_DIM + ROPE_DIM
    n_heads = p["w_q_up"].shape[2] // qk_dim
    w_in = p["w_in"][i]
    o_kr = q_rank + kv_rank
    w_in_ext = jnp.concatenate(
        [w_in[:, :o_kr + ROPE_DIM], _rot_columns(w_in[:, o_kr:o_kr + ROPE_DIM]), w_in[:, o_kr + ROPE_DIM:]],
        axis=1)
    wq = p["w_q_up"][i].reshape(q_rank, n_heads, qk_dim)
    wq_ext = jnp.concatenate(
        [wq, jax.vmap(_rot_columns, in_axes=1, out_axes=1)(wq[:, :, NOPE_DIM:])], axis=2)
    swap = _half_swap(ROPE_DIM)
    d_ff = p["ff_conv"].shape[2]
    nc = d_ff // FF_CHUNK
    ff_up = p["ff_up"][i]

    def chunked_cols(w):
        return w.reshape(w.shape[0], nc, FF_CHUNK).transpose(1, 0, 2)

    pool_w = p["pool_w"][i]
    n_groups, pool_ch = pool_w.shape[0], pool_w.shape[1]
    pool_bd = jnp.zeros((side, side), F32)
    for g in range(n_groups):
        pool_bd = pool_bd.at[g * pool_ch:(g + 1) * pool_ch, g * pool_ch:(g + 1) * pool_ch].set(pool_w[g])

    def row(v):
        return v.reshape(1, -1)

    return dict(
        n_heads=n_heads, q_rank=q_rank, kv_rank=kv_rank, side=side,
        scale=float(qk_dim) ** -0.5 * float(np.log2(np.e)),
        norm1_g=row(p["norm1_g"][i]), norm2_g=row(p["norm2_g"][i]),
        w_in=w_in_ext.astype(BF16), qc_g=row(p["qc_g"][i]), kvc_g=row(p["kvc_g"][i]),
        w_q=wq_ext.reshape(q_rank, n_heads * HEAD_PAD).astype(BF16),
        w_kv=p["w_kv_up"][i].astype(BF16),
        qn_g=row(p["qn_g"][i]), kn_g=row(p["kn_g"][i]),
        qr_g=row(jnp.concatenate([p["qr_g"][i], p["qr_g"][i][swap]])),
        kr_g=row(jnp.concatenate([p["kr_g"][i], p["kr_g"][i][swap]])),
        pool_w=pool_bd.astype(BF16), pool_scale=row(p["pool_scale"][i]), sconv_w=p["sconv_w"][i],
        w_o=p["w_o"][i].astype(BF16),
        ff_g=chunked_cols(ff_up[:, :d_ff]).astype(BF16), ff_u=chunked_cols(ff_up[:, d_ff:]).astype(BF16),
        ff_conv=p["ff_conv"][i].reshape(CONV_K, nc, FF_CHUNK).transpose(1, 0, 2),
        ff_down=p["ff_down"][i].reshape(nc, FF_CHUNK, d).astype(BF16),
    )


def _rope_table(n):
    rows = n // GRID_W
    row = jnp.repeat(jnp.arange(rows), GRID_W).astype(F32)
    col = jnp.tile(jnp.arange(GRID_W), rows).astype(F32)
    n_freq = ROPE_DIM // 4
    inv = ROPE_THETA ** (-jnp.arange(n_freq, dtype=F32) / n_freq)
    ar = row[:, None] * inv
    ac = col[:, None] * inv
    ang = jnp.concatenate([ar, ar, ac, ac], axis=-1)
    return jnp.concatenate([jnp.cos(ang), jnp.sin(ang)], axis=-1)


def kernel(x, c, ctx, c_ctx, ada_w, ada_b, norm1_g, norm2_g, w_in, qc_g, kvc_g, w_q_up, w_kv_up,
           qn_g, qr_g, kn_g, kr_g, pool_w, pool_scale, sconv_w, w_o, ff_up, ff_conv, ff_down):
    params = dict(norm1_g=norm1_g, norm2_g=norm2_g, w_in=w_in, qc_g=qc_g, kvc_g=kvc_g,
                  w_q_up=w_q_up, w_kv_up=w_kv_up, qn_g=qn_g, qr_g=qr_g, kn_g=kn_g, kr_g=kr_g,
                  pool_w=pool_w, pool_scale=pool_scale, sconv_w=sconv_w, w_o=w_o, ff_up=ff_up,
                  ff_conv=ff_conv, ff_down=ff_down)
    depth = ada_w.shape[0]
    b, n, d = x.shape
    n_ctx = ctx.shape[1]
    tm = min(ROW_TILE, n)
    tq = min(Q_TILE, n)

    mod_rows = -(-(b + 1) // SUBLANES) * SUBLANES
    cvec = jnp.zeros((mod_rows, d), F32).at[:b].set(c).at[b].set(c_ctx)
    mods = _adaln(cvec, ada_w, ada_b).reshape(depth, mod_rows, N_MOD, d)

    cs_x = _rope_table(n)
    cs_c = jnp.concatenate([jnp.ones((n_ctx, ROPE_DIM), F32), jnp.zeros((n_ctx, ROPE_DIM), F32)], axis=-1)

    for i in range(depth):
        lw = _layer_weights(i, params)
        mod_x = mods[i, :b]
        mod_c = mods[i, b:b + 1]
        qT_x, k_x, vT_x, rest_x = _pre(x, mod_x, lw, cs_x, tm=tm)
        qT_c, k_c, vT_c, rest_c = _pre(ctx, mod_c, lw, cs_c, tm=n_ctx)
        nh = lw["n_heads"]
        src_x = (k_x.reshape(b, nh, n // tm, tm, HEAD_PAD), vT_x)
        src_c = (k_c.reshape(b, nh, 1, n_ctx, HEAD_PAD), vT_c)
        attn_x = _attn(qT_x, [src_x, src_c], tq=tq)
        x = _mix(x, attn_x, rest_x, mod_x, lw, tm=tm)
        x = _ffn(x, mod_x, lw, tm=tm)
        if i < depth - 1:
            attn_c = _attn(qT_c, [src_c], tq=min(tq, n_ctx))
            ctx = _mix(ctx, attn_c, rest_c, mod_c, lw, tm=n_ctx)
            ctx = _ffn(ctx, mod_c, lw, tm=n_ctx)
    return x
```

```python
import functools

import jax
import jax.numpy as jnp
import numpy as np
from jax import lax
from jax.experimental import pallas as pl
from jax.experimental.pallas import tpu as pltpu

F32 = jnp.float32
BF16 = jnp.bfloat16

GRID_W = 64
NOPE_DIM = 128
ROPE_DIM = 64
V_DIM = 128
ROPE_THETA = 10000.0
POOL_WINDOWS = (2, 4, 8, 16)
CONV_K = 3
EPS = 1e-6
N_MOD = 6

LANES = 128
SUBLANES = 8
MXU_DIM = 256
HEAD_PAD = MXU_DIM
HALO = SUBLANES
VMEM_LIMIT = 52 * 1024 * 1024

ROW_TILE = 512
Q_TILE = 256
Q_TILES_PER_STEP = 4
FF_CHUNK = 256
ATTN_LOOKAHEAD = 4


def _rms(x, width=None):
    width = x.shape[-1] if width is None else width
    return x * lax.rsqrt(jnp.sum(x * x, axis=-1, keepdims=True) * (1.0 / width) + EPS)


def _sigmoid(x):
    return 1.0 / (1.0 + jnp.exp(-x))


def _dot(a, b):
    return jnp.dot(a, b, preferred_element_type=F32)


def _adaln_kernel(c_ref, w_ref, b_ref, o_ref):
    c = c_ref[...]
    s = (c * _sigmoid(c)).astype(BF16)
    o_ref[0] = _dot(s, w_ref[0].astype(BF16)) + b_ref[0]


def _adaln(cvec, ada_w, ada_b):
    depth, d, width = ada_w.shape
    rows = cvec.shape[0]
    tn = width // 4
    return pl.pallas_call(
        _adaln_kernel,
        out_shape=jax.ShapeDtypeStruct((depth, rows, width), F32),
        grid=(depth, width // tn),
        in_specs=[
            pl.BlockSpec((rows, d), lambda i, n: (0, 0)),
            pl.BlockSpec((1, d, tn), lambda i, n: (i, 0, n)),
            pl.BlockSpec((1, 1, tn), lambda i, n: (i, 0, n)),
        ],
        out_specs=pl.BlockSpec((1, rows, tn), lambda i, n: (i, 0, n)),
        compiler_params=pltpu.CompilerParams(
            dimension_semantics=("arbitrary", "arbitrary"), vmem_limit_bytes=VMEM_LIMIT),
        name="adaln",
    )(cvec, ada_w, ada_b.reshape(depth, 1, width))


def _pre_kernel(x_ref, mod_ref, ng_ref, w_in_ref, qcg_ref, kvcg_ref, wqT_ref, wkn_ref, wvT_ref,
                qng_ref, qrg_ref, kng_ref, krg_ref, cs_ref, csT_ref,
                qT_ref, k_ref, vT_ref, rest_ref, *, n_heads, q_rank, kv_rank, side, scale):
    tm = x_ref.shape[1]
    n_sub = 2 if tm % (2 * MXU_DIM) == 0 else 1
    sub_rows = tm // n_sub
    subs = [slice(r * sub_rows, (r + 1) * sub_rows) for r in range(n_sub)]
    o_kv = q_rank
    o_kr = o_kv + kv_rank
    o_pool = o_kr + 2 * ROPE_DIM
    lane = lax.broadcasted_iota(jnp.int32, (1, LANES), 1)
    keep = (lane < ROPE_DIM).astype(F32)

    def key_rope(tile, rows):
        t = (_rms(tile) * krg_ref[...]) * cs_ref[rows, :]
        return (t + pltpu.roll(t, ROPE_DIM, 1)) * keep

    def lane_tiled(ref):
        return jnp.tile(ref[...], (1, sub_rows // LANES))

    def col_rms(a):
        return lax.rsqrt(jnp.sum(a * a, axis=0, keepdims=True) * (1.0 / a.shape[0]) + EPS)

    hb, lat = [], []
    for rows in subs:
        h = _rms(x_ref[0, rows, :]) * ng_ref[...]
        h = h * (1.0 + mod_ref[0, 1:2, :]) + mod_ref[0, 0:1, :]
        hb.append(h.astype(BF16))
        lat.append(_dot(hb[-1], w_in_ref[:, 0:o_pool]))

    for rows, hs in zip(subs, hb):
        rest_ref[0, rows, 0:2 * side] = _dot(hs, w_in_ref[:, o_pool:o_pool + 2 * side])
        gc_cv = _dot(hs, w_in_ref[:, o_pool + 2 * side:o_pool + 4 * side])
        rest_ref[0, rows, 2 * side:3 * side] = gc_cv[:, :side] * gc_cv[:, side:]

    qcT = [(_rms(t[:, 0:q_rank]) * qcg_ref[...]).T.astype(BF16) for t in lat]
    kvc = [_rms(t[:, o_kv:o_kr]) * kvcg_ref[...] for t in lat]
    kvcT = [t.T.astype(BF16) for t in kvc]
    kvc = [t.astype(BF16) for t in kvc]
    qT_all = [_dot(wqT_ref[...], t) for t in qcT]
    kn_all = [_dot(t, wkn_ref[...]) for t in kvc]
    vT_all = [_dot(wvT_ref[...], t) for t in kvcT]
    qn_gain, qr_gain = lane_tiled(qng_ref), lane_tiled(qrg_ref)
    for r, rows in enumerate(subs):
        kr = key_rope(lat[r][:, o_kr:o_pool], rows).astype(BF16)
        csT = csT_ref[:, rows]
        for hd in range(n_heads):
            kn = kn_all[r][:, hd * NOPE_DIM:(hd + 1) * NOPE_DIM]
            k_ref[0, hd, rows, :NOPE_DIM] = (_rms(kn) * kng_ref[...]).astype(BF16)
            k_ref[0, hd, rows, NOPE_DIM:] = kr
            vT_ref[0, hd, 0, :, rows] = vT_all[r][hd * V_DIM:(hd + 1) * V_DIM].astype(BF16)
            q = qT_all[r][hd * HEAD_PAD:(hd + 1) * HEAD_PAD]
            nope, pair = q[:NOPE_DIM], q[NOPE_DIM:]
            qn = ((nope * col_rms(nope)) * qn_gain) * scale
            t = ((pair * col_rms(pair[:ROPE_DIM])) * qr_gain) * csT
            qr = (t[:ROPE_DIM] + t[ROPE_DIM:]) * scale
            qT_ref[0, hd, 0:NOPE_DIM, rows] = qn.astype(BF16)
            qT_ref[0, hd, NOPE_DIM:NOPE_DIM + ROPE_DIM, rows] = qr.astype(BF16)
            qT_ref[0, hd, NOPE_DIM + ROPE_DIM:, rows] = jnp.zeros((HEAD_PAD - NOPE_DIM - ROPE_DIM, sub_rows), BF16)


def _const_spec(shape):
    nd = len(shape)
    return pl.BlockSpec(shape, lambda *_: (0,) * nd, pipeline_mode=pl.Buffered(1))


def _pre(x, mod, lw, cs, *, tm):
    b, n, d = x.shape
    n_heads = lw["n_heads"]
    side = lw["side"]
    nt = n // tm
    mod_rows = mod.shape[0]
    mod_map = (lambda bi, j: (bi, 0, 0)) if mod_rows > 1 else (lambda bi, j: (0, 0, 0))
    kern = functools.partial(_pre_kernel, n_heads=n_heads, q_rank=lw["q_rank"],
                             kv_rank=lw["kv_rank"], side=side, scale=lw["scale"])
    consts = [lw["norm1_g"], lw["w_in"], lw["qc_g"], lw["kvc_g"], lw["w_qT"], lw["w_kn"], lw["w_vT"],
              lw["qn_g"], lw["qr_g"], lw["kn_g"], lw["kr_g"]]
    cs, csT = cs
    return pl.pallas_call(
        kern,
        out_shape=(
            jax.ShapeDtypeStruct((b, n_heads, HEAD_PAD, n), BF16),
            jax.ShapeDtypeStruct((b, n_heads, n, HEAD_PAD), BF16),
            jax.ShapeDtypeStruct((b, n_heads, nt, V_DIM, tm), BF16),
            jax.ShapeDtypeStruct((b, n, 3 * side), F32),
        ),
        grid=(b, nt),
        in_specs=[pl.BlockSpec((1, tm, d), lambda bi, j: (bi, j, 0)),
                  pl.BlockSpec((1, N_MOD, d), mod_map)]
                 + [_const_spec(c.shape) for c in consts]
                 + [pl.BlockSpec((tm, LANES), lambda bi, j: (j, 0)),
                    pl.BlockSpec((LANES, tm), lambda bi, j: (0, j))],
        out_specs=(
            pl.BlockSpec((1, n_heads, HEAD_PAD, tm), lambda bi, j: (bi, 0, 0, j)),
            pl.BlockSpec((1, n_heads, tm, HEAD_PAD), lambda bi, j: (bi, 0, j, 0)),
            pl.BlockSpec((1, n_heads, 1, V_DIM, tm), lambda bi, j: (bi, 0, j, 0, 0)),
            pl.BlockSpec((1, tm, 3 * side), lambda bi, j: (bi, j, 0)),
        ),
        compiler_params=pltpu.CompilerParams(
            dimension_semantics=("arbitrary", "arbitrary"), vmem_limit_bytes=VMEM_LIMIT),
        name="pre_proj",
    )(x, mod, *consts, cs, csT)


def _col_reduce(x, op):
    rows = x.shape[0]
    while rows % (2 * SUBLANES) == 0:
        rows //= 2
        x = op(x[:rows], x[rows:])
    if op is jnp.add:
        return jnp.sum(x, axis=0, keepdims=True)
    return jnp.max(x, axis=0, keepdims=True)


def _attn_kernel(qT_ref, *refs, n_chunks, tq):
    o_ref = refs[-1]
    n_tiles = qT_ref.shape[3] // tq
    items = [(t, refs[2 * src], refs[2 * src + 1], c)
             for t in range(n_tiles) for src, nc in enumerate(n_chunks) for c in range(nc)]
    per_tile = len(items) // n_tiles

    n = len(items)
    ahead = min(ATTN_LOOKAHEAD, n)
    pending = []
    m_run = None

    def issue(i):
        nonlocal m_run
        t, k_ref, _, c = items[i]
        s = _dot(k_ref[0, 0, c], qT_ref[0, 0, :, t * tq:(t + 1) * tq])
        cm = _col_reduce(s, jnp.maximum)
        if i % per_tile == 0:
            m_new, alpha = cm, None
        else:
            m_new = jnp.maximum(m_run, cm)
            alpha = jnp.exp2(m_run - m_new)
        pending.append((s, m_new, alpha))
        m_run = m_new

    for i in range(ahead):
        issue(i)
    l = acc = None
    for i, (t, _, vT_ref, c) in enumerate(items):
        if i + ahead < n:
            issue(i + ahead)
        s, m, alpha = pending.pop(0)
        p = jnp.exp2(s - m)
        p_sum = _col_reduce(p, jnp.add)
        pv = _dot(vT_ref[0, 0, c], p.astype(BF16))
        if alpha is None:
            l, acc = p_sum, pv
        else:
            l, acc = alpha * l + p_sum, alpha * acc + pv
        if (i + 1) % per_tile == 0:
            o_ref[0, t * tq:(t + 1) * tq, :] = (acc * (1.0 / l)).T.astype(BF16)


def _attn(qT, sources, *, tq, tiles_per_step):
    b, n_heads, _, n = qT.shape
    tq_step = tq * tiles_per_step
    in_specs = [pl.BlockSpec((1, 1, HEAD_PAD, tq_step), lambda bi, hi, i: (bi, hi, 0, i))]
    args = [qT]
    n_chunks = []
    for k, vT in sources:
        nc, tk = k.shape[2], k.shape[3]
        n_chunks.append(nc)
        in_specs.append(pl.BlockSpec((1, 1, nc, tk, HEAD_PAD), lambda bi, hi, i: (bi, hi, 0, 0, 0)))
        in_specs.append(pl.BlockSpec((1, 1, nc, V_DIM, tk), lambda bi, hi, i: (bi, hi, 0, 0, 0)))
        args += [k, vT]
    return pl.pallas_call(
        functools.partial(_attn_kernel, n_chunks=tuple(n_chunks), tq=tq),
        out_shape=jax.ShapeDtypeStruct((b, n, n_heads * V_DIM), BF16),
        grid=(b, n_heads, n // tq_step),
        in_specs=in_specs,
        out_specs=pl.BlockSpec((1, tq_step, V_DIM), lambda bi, hi, i: (bi, i, hi)),
        compiler_params=pltpu.CompilerParams(
            dimension_semantics=("arbitrary", "arbitrary", "arbitrary"),
            vmem_limit_bytes=VMEM_LIMIT),
        name="flash_attn",
    )(*args)


def _halo_specs(tm, width, nt):
    per = tm // HALO

    def prev_map(bi, j):
        return (bi, jnp.maximum(j * per - 1, 0), 0)

    def next_map(bi, j):
        return (bi, jnp.minimum((j + 1) * per, nt * per - 1), 0)

    return [pl.BlockSpec((1, tm, width), lambda bi, j: (bi, j, 0)),
            pl.BlockSpec((1, HALO, width), prev_map),
            pl.BlockSpec((1, HALO, width), next_map)]


def _shift_rows(a, s):
    return pltpu.roll(a, s % a.shape[0], 0)


def _mix_kernel(x_ref, attn_ref, rest_ref, rprev_ref, rnext_ref, mod_ref, pool_w_ref,
                pool_scale_ref, sconv_ref, wo_ref, o_ref, *, side, attn_width, nt):
    j = pl.program_id(1)
    tm = x_ref.shape[1]
    seq_len = tm * nt
    rest = rest_ref[0]
    mix_attn = _dot(attn_ref[0], wo_ref[0:attn_width, :])
    prev = jnp.where(j == 0, 0.0, rprev_ref[0])
    nxt = jnp.where(j == nt - 1, 0.0, rnext_ref[0])
    ext = jnp.concatenate([prev, rest, nxt], axis=0)
    main = slice(HALO, HALO + tm)

    pe = ext[:, 0:side]
    w2 = pe + _shift_rows(pe, 1)
    w4 = _shift_rows(w2, 1) + _shift_rows(w2, -1)
    w8 = _shift_rows(w4, 2) + _shift_rows(w4, -2)
    w16 = _shift_rows(w8, 4) + _shift_rows(w8, -4)
    n_groups = len(POOL_WINDOWS)
    grp = lax.broadcasted_iota(jnp.int32, (1, side), 1) // (side // n_groups)
    total = jnp.where(grp == 0, w2[main], jnp.where(grp == 1, w4[main],
                      jnp.where(grp == 2, w8[main], w16[main])))
    half = jnp.where(grp == 0, POOL_WINDOWS[0] // 2, jnp.where(grp == 1, POOL_WINDOWS[1] // 2,
                     jnp.where(grp == 2, POOL_WINDOWS[2] // 2, POOL_WINDOWS[3] // 2)))
    t = j * tm + lax.broadcasted_iota(jnp.int32, (tm, 1), 0)
    count = jnp.minimum(t + half, seq_len) - jnp.maximum(t - half, 0)
    diff = total / count.astype(F32) - rest[:, 0:side]
    pool_out = _dot(diff.astype(BF16), pool_w_ref[...]) * pool_scale_ref[...]

    ue = ext[:, 2 * side:3 * side]
    conv = (_shift_rows(ue, 1)[main] * sconv_ref[0:1, :] + rest[:, 2 * side:3 * side] * sconv_ref[1:2, :]
            + _shift_rows(ue, -1)[main] * sconv_ref[2:3, :])
    conv_out = rest[:, side:2 * side] * conv

    mix = (mix_attn
           + _dot(pool_out.astype(BF16), wo_ref[attn_width:attn_width + side, :])
           + _dot(conv_out.astype(BF16), wo_ref[attn_width + side:, :]))
    o_ref[0] = x_ref[0] + mod_ref[0, 2:3, :] * mix


def _mix(x, attn, rest, mod, lw, *, tm):
    b, n, d = x.shape
    side = lw["side"]
    attn_width = attn.shape[-1]
    mod_rows = mod.shape[0]
    mod_map = (lambda bi, j: (bi, 0, 0)) if mod_rows > 1 else (lambda bi, j: (0, 0, 0))
    consts = [lw["pool_w"], lw["pool_scale"], lw["sconv_w"], lw["w_o"]]
    nt = n // tm
    return pl.pallas_call(
        functools.partial(_mix_kernel, side=side, attn_width=attn_width, nt=nt),
        out_shape=jax.ShapeDtypeStruct((b, n, d), F32),
        grid=(b, nt),
        in_specs=[pl.BlockSpec((1, tm, d), lambda bi, j: (bi, j, 0)),
                  pl.BlockSpec((1, tm, attn_width), lambda bi, j: (bi, j, 0))]
                 + _halo_specs(tm, 3 * side, nt)
                 + [pl.BlockSpec((1, N_MOD, d), mod_map)]
                 + [_const_spec(c.shape) for c in consts],
        out_specs=pl.BlockSpec((1, tm, d), lambda bi, j: (bi, j, 0)),
        compiler_params=pltpu.CompilerParams(
            dimension_semantics=("arbitrary", "arbitrary"), vmem_limit_bytes=VMEM_LIMIT),
        name="mixer_out",
    )(x, attn, rest, rest, rest, mod, *consts)


def _ffn_kernel(x_ref, xprev_ref, xnext_ref, mod_ref, ng_ref, wg_ref, wu_ref, wconv_ref, wd_ref,
                o_ref, act_ref, *, nt):
    j = pl.program_id(1)
    tm = x_ref.shape[1]
    nc, _, cw = wg_ref.shape
    x = x_ref[0]
    xe = jnp.concatenate([xprev_ref[0], x, xnext_ref[0]], axis=0)
    he = _rms(xe) * ng_ref[...]
    he = he * (1.0 + mod_ref[0, 4:5, :]) + mod_ref[0, 3:4, :]
    main = slice(HALO, HALO + tm)
    hm = he[main].astype(BF16)
    he = he.astype(BF16)

    def up(c):
        return _dot(he, wg_ref[c]), _dot(hm, wu_ref[c])

    nxt = up(0)
    for c in range(nc):
        g, u = nxt
        if c + 1 < nc:
            nxt = up(c + 1)
        g = jnp.concatenate([jnp.where(j > 0, g[:HALO], 0.0), g[main],
                             jnp.where(j < nt - 1, g[HALO + tm:], 0.0)], axis=0)
        w = wconv_ref[c]
        cv = (_shift_rows(g, 1)[main] * w[0:1, :] + g[main] * w[1:2, :]
              + _shift_rows(g, -1)[main] * w[2:3, :])
        act_ref[:, c * cw:(c + 1) * cw] = ((cv * _sigmoid(cv)) * u).astype(BF16)

    o_ref[0] = x + mod_ref[0, 5:6, :] * _dot(act_ref[...], wd_ref[...])


def _ffn(x, mod, lw, *, tm):
    b, n, d = x.shape
    mod_rows = mod.shape[0]
    mod_map = (lambda bi, j: (bi, 0, 0)) if mod_rows > 1 else (lambda bi, j: (0, 0, 0))
    consts = [lw["norm2_g"], lw["ff_g"], lw["ff_u"], lw["ff_conv"], lw["ff_down"]]
    nt = n // tm
    return pl.pallas_call(
        functools.partial(_ffn_kernel, nt=nt),
        out_shape=jax.ShapeDtypeStruct((b, n, d), F32),
        grid=(b, nt),
        in_specs=_halo_specs(tm, d, nt)
                 + [pl.BlockSpec((1, N_MOD, d), mod_map)]
                 + [_const_spec(c.shape) for c in consts],
        out_specs=pl.BlockSpec((1, tm, d), lambda bi, j: (bi, j, 0)),
        scratch_shapes=[pltpu.VMEM((tm, lw["ff_down"].shape[0]), BF16)],
        compiler_params=pltpu.CompilerParams(
            dimension_semantics=("arbitrary", "arbitrary"), vmem_limit_bytes=VMEM_LIMIT),
        name="conv_ffn",
    )(x, x, x, mod, *consts)


def _half_swap(n):
    quarter = ROPE_DIM // 4
    idx = np.arange(n).reshape(-1, 2, quarter)
    return idx[:, ::-1, :].reshape(-1)


def _rot_columns(w):
    quarter = ROPE_DIM // 4
    sign = np.tile(np.concatenate([-np.ones(quarter), np.ones(quarter)]), ROPE_DIM // (2 * quarter))
    return w[:, _half_swap(ROPE_DIM)] * jnp.asarray(sign, w.dtype)


def _layer_weights(i, p):
    d = p["w_in"].shape[1]
    q_rank = p["qc_g"].shape[1]
    kv_rank = p["kvc_g"].shape[1]
    side = p["pool_scale"].shape[1]
    qk_dim = NOPE_DIM + ROPE_DIM
    n_heads = p["w_q_up"].shape[2] // qk_dim
    w_in = p["w_in"][i]
    o_kr = q_rank + kv_rank
    w_in_ext = jnp.concatenate(
        [w_in[:, :o_kr + ROPE_DIM], _rot_columns(w_in[:, o_kr:o_kr + ROPE_DIM]), w_in[:, o_kr + ROPE_DIM:]],
        axis=1)
    wq = p["w_q_up"][i].reshape(q_rank, n_heads, qk_dim)
    wq_ext = jnp.concatenate(
        [wq, jax.vmap(_rot_columns, in_axes=1, out_axes=1)(wq[:, :, NOPE_DIM:])], axis=2)
    swap = _half_swap(ROPE_DIM)
    d_ff = p["ff_conv"].shape[2]
    nc = d_ff // FF_CHUNK
    ff_up = p["ff_up"][i]

    def chunked_cols(w):
        return w.reshape(w.shape[0], nc, FF_CHUNK).transpose(1, 0, 2)

    pool_w = p["pool_w"][i]
    n_groups, pool_ch = pool_w.shape[0], pool_w.shape[1]
    pool_bd = jnp.zeros((side, side), F32)
    for g in range(n_groups):
        pool_bd = pool_bd.at[g * pool_ch:(g + 1) * pool_ch, g * pool_ch:(g + 1) * pool_ch].set(pool_w[g])

    def row(v):
        return v.reshape(1, -1)

    def lane_bcast(v):
        return jnp.broadcast_to(v[:, None], (v.shape[0], LANES))

    w_kv = p["w_kv_up"][i].reshape(kv_rank, n_heads, NOPE_DIM + V_DIM)

    return dict(
        n_heads=n_heads, q_rank=q_rank, kv_rank=kv_rank, side=side,
        scale=float(qk_dim) ** -0.5 * float(np.log2(np.e)),
        norm1_g=row(p["norm1_g"][i]), norm2_g=row(p["norm2_g"][i]),
        w_in=w_in_ext.astype(BF16), qc_g=row(p["qc_g"][i]), kvc_g=row(p["kvc_g"][i]),
        w_qT=wq_ext.reshape(q_rank, n_heads * HEAD_PAD).T.astype(BF16),
        w_kn=w_kv[:, :, :NOPE_DIM].reshape(kv_rank, n_heads * NOPE_DIM).astype(BF16),
        w_vT=w_kv[:, :, NOPE_DIM:].reshape(kv_rank, n_heads * V_DIM).T.astype(BF16),
        qn_g=lane_bcast(p["qn_g"][i]), kn_g=row(p["kn_g"][i]),
        qr_g=lane_bcast(jnp.concatenate([p["qr_g"][i], p["qr_g"][i][swap]])),
        kr_g=row(jnp.concatenate([p["kr_g"][i], p["kr_g"][i][swap]])),
        pool_w=pool_bd.astype(BF16), pool_scale=row(p["pool_scale"][i]), sconv_w=p["sconv_w"][i],
        w_o=p["w_o"][i].astype(BF16),
        ff_g=chunked_cols(ff_up[:, :d_ff]).astype(BF16), ff_u=chunked_cols(ff_up[:, d_ff:]).astype(BF16),
        ff_conv=p["ff_conv"][i].reshape(CONV_K, nc, FF_CHUNK).transpose(1, 0, 2),
        ff_down=p["ff_down"][i].astype(BF16),
    )


def _rope_table(n):
    rows = n // GRID_W
    row = jnp.repeat(jnp.arange(rows), GRID_W).astype(F32)
    col = jnp.tile(jnp.arange(GRID_W), rows).astype(F32)
    n_freq = ROPE_DIM // 4
    inv = ROPE_THETA ** (-jnp.arange(n_freq, dtype=F32) / n_freq)
    ar = row[:, None] * inv
    ac = col[:, None] * inv
    ang = jnp.concatenate([ar, ar, ac, ac], axis=-1)
    return jnp.concatenate([jnp.cos(ang), jnp.sin(ang)], axis=-1)


def kernel(x, c, ctx, c_ctx, ada_w, ada_b, norm1_g, norm2_g, w_in, qc_g, kvc_g, w_q_up, w_kv_up,
           qn_g, qr_g, kn_g, kr_g, pool_w, pool_scale, sconv_w, w_o, ff_up, ff_conv, ff_down):
    params = dict(norm1_g=norm1_g, norm2_g=norm2_g, w_in=w_in, qc_g=qc_g, kvc_g=kvc_g,
                  w_q_up=w_q_up, w_kv_up=w_kv_up, qn_g=qn_g, qr_g=qr_g, kn_g=kn_g, kr_g=kr_g,
                  pool_w=pool_w, pool_scale=pool_scale, sconv_w=sconv_w, w_o=w_o, ff_up=ff_up,
                  ff_conv=ff_conv, ff_down=ff_down)
    depth = ada_w.shape[0]
    b, n, d = x.shape
    n_ctx = ctx.shape[1]
    tm = min(ROW_TILE, n)
    tq = min(Q_TILE, n)

    mod_rows = -(-(b + 1) // SUBLANES) * SUBLANES
    cvec = jnp.zeros((mod_rows, d), F32).at[:b].set(c).at[b].set(c_ctx)
    mods = _adaln(cvec, ada_w, ada_b).reshape(depth, mod_rows, N_MOD, d)

    cs_x = _rope_table(n)
    cs_c = jnp.concatenate([jnp.ones((n_ctx, ROPE_DIM), F32), jnp.zeros((n_ctx, ROPE_DIM), F32)], axis=-1)
    cs_x, cs_c = (cs_x, cs_x.T), (cs_c, cs_c.T)

    for i in range(depth):
        lw = _layer_weights(i, params)
        mod_x = mods[i, :b]
        mod_c = mods[i, b:b + 1]
        qT_x, k_x, vT_x, rest_x = _pre(x, mod_x, lw, cs_x, tm=tm)
        qT_c, k_c, vT_c, rest_c = _pre(ctx, mod_c, lw, cs_c, tm=n_ctx)
        nh = lw["n_heads"]
        src_x = (k_x.reshape(b, nh, n // tm, tm, HEAD_PAD), vT_x)
        src_c = (k_c.reshape(b, nh, 1, n_ctx, HEAD_PAD), vT_c)
        attn_x = _attn(qT_x, [src_x, src_c], tq=tq, tiles_per_step=min(Q_TILES_PER_STEP, n // tq))
        x = _mix(x, attn_x, rest_x, mod_x, lw, tm=tm)
        x = _ffn(x, mod_x, lw, tm=tm)
        if i < depth - 1:
            attn_c = _attn(qT_c, [src_c], tq=min(tq, n_ctx), tiles_per_step=1)
            ctx = _mix(ctx, attn_c, rest_c, mod_c, lw, tm=n_ctx)
            ctx = _ffn(ctx, mod_c, lw, tm=n_ctx)
    return x
```

```python
import functools

import jax
import jax.numpy as jnp
import numpy as np
from jax import lax
from jax.experimental import pallas as pl
from jax.experimental.pallas import tpu as pltpu

F32 = jnp.float32
BF16 = jnp.bfloat16

GRID_W = 64
NOPE_DIM = 128
ROPE_DIM = 64
V_DIM = 128
ROPE_THETA = 10000.0
POOL_WINDOWS = (2, 4, 8, 16)
CONV_K = 3
EPS = 1e-6
N_MOD = 6

LANES = 128
SUBLANES = 8
MXU_DIM = 256
HEAD_PAD = MXU_DIM
BIAS_FEATURE = NOPE_DIM + ROPE_DIM
HALO = SUBLANES
VMEM_LIMIT = 52 * 1024 * 1024

ROW_TILE = 512
Q_TILE = 256
Q_TILES_PER_STEP = 4
FF_CHUNK = 256
ATTN_LOOKAHEAD = 4
FIXED_SHIFT_LOOKAHEAD = 1


def _rms(x, width=None):
    width = x.shape[-1] if width is None else width
    return x * lax.rsqrt(jnp.sum(x * x, axis=-1, keepdims=True) * (1.0 / width) + EPS)


def _sigmoid(x):
    return 1.0 / (1.0 + jnp.exp(-x))


def _dot(a, b):
    return jnp.dot(a, b, preferred_element_type=F32)


def _adaln_kernel(c_ref, w_ref, b_ref, o_ref):
    c = c_ref[...]
    s = (c * _sigmoid(c)).astype(BF16)
    o_ref[0] = _dot(s, w_ref[0].astype(BF16)) + b_ref[0]


def _adaln(cvec, ada_w, ada_b):
    depth, d, width = ada_w.shape
    rows = cvec.shape[0]
    tn = width // 4
    return pl.pallas_call(
        _adaln_kernel,
        out_shape=jax.ShapeDtypeStruct((depth, rows, width), F32),
        grid=(depth, width // tn),
        in_specs=[
            pl.BlockSpec((rows, d), lambda i, n: (0, 0)),
            pl.BlockSpec((1, d, tn), lambda i, n: (i, 0, n)),
            pl.BlockSpec((1, 1, tn), lambda i, n: (i, 0, n)),
        ],
        out_specs=pl.BlockSpec((1, rows, tn), lambda i, n: (i, 0, n)),
        compiler_params=pltpu.CompilerParams(
            dimension_semantics=("arbitrary", "arbitrary"), vmem_limit_bytes=VMEM_LIMIT),
        name="adaln",
    )(cvec, ada_w, ada_b.reshape(depth, 1, width))


def _pre_kernel(x_ref, mod_ref, ng_ref, w_in_ref, qcg_ref, kvcg_ref, wqT_ref, wkn_ref, wvT_ref,
                qng_ref, qrg_ref, kng_ref, krg_ref, cs_ref, csT_ref,
                qT_ref, k_ref, vT_ref, rest_ref, *, n_heads, q_rank, kv_rank, side, scale):
    tm = x_ref.shape[1]
    n_sub = 2 if tm % (2 * MXU_DIM) == 0 else 1
    sub_rows = tm // n_sub
    subs = [slice(r * sub_rows, (r + 1) * sub_rows) for r in range(n_sub)]
    o_kv = q_rank
    o_kr = o_kv + kv_rank
    o_pool = o_kr + 2 * ROPE_DIM
    lane = lax.broadcasted_iota(jnp.int32, (1, LANES), 1)
    keep = (lane < ROPE_DIM).astype(F32)
    bias_lane = (lane == BIAS_FEATURE - NOPE_DIM).astype(F32)

    def key_rope(tile, rows):
        t = (_rms(tile) * krg_ref[...]) * cs_ref[rows, :]
        return (t + pltpu.roll(t, ROPE_DIM, 1)) * keep + bias_lane

    def lane_tiled(ref):
        return jnp.tile(ref[...], (1, sub_rows // LANES))

    def col_rms(a):
        return lax.rsqrt(jnp.sum(a * a, axis=0, keepdims=True) * (1.0 / a.shape[0]) + EPS)

    hb, lat = [], []
    for rows in subs:
        h = _rms(x_ref[0, rows, :]) * ng_ref[...]
        h = h * (1.0 + mod_ref[0, 1:2, :]) + mod_ref[0, 0:1, :]
        hb.append(h.astype(BF16))
        lat.append(_dot(hb[-1], w_in_ref[:, 0:o_pool]))

    for rows, hs in zip(subs, hb):
        rest_ref[0, rows, 0:2 * side] = _dot(hs, w_in_ref[:, o_pool:o_pool + 2 * side])
        gc_cv = _dot(hs, w_in_ref[:, o_pool + 2 * side:o_pool + 4 * side])
        rest_ref[0, rows, 2 * side:3 * side] = gc_cv[:, :side] * gc_cv[:, side:]

    qcT = [(_rms(t[:, 0:q_rank]) * qcg_ref[...]).T.astype(BF16) for t in lat]
    kvc = [_rms(t[:, o_kv:o_kr]) * kvcg_ref[...] for t in lat]
    kvcT = [t.T.astype(BF16) for t in kvc]
    kvc = [t.astype(BF16) for t in kvc]
    qT_all = [_dot(wqT_ref[...], t) for t in qcT]
    kn_all = [_dot(t, wkn_ref[...]) for t in kvc]
    vT_all = [_dot(wvT_ref[...], t) for t in kvcT]
    qn_gain, qr_gain = lane_tiled(qng_ref), lane_tiled(qrg_ref)
    for r, rows in enumerate(subs):
        kr = key_rope(lat[r][:, o_kr:o_pool], rows).astype(BF16)
        csT = csT_ref[:, rows]
        for hd in range(n_heads):
            kn = kn_all[r][:, hd * NOPE_DIM:(hd + 1) * NOPE_DIM]
            k_ref[0, hd, rows, :NOPE_DIM] = (_rms(kn) * kng_ref[...]).astype(BF16)
            k_ref[0, hd, rows, NOPE_DIM:] = kr
            vT_ref[0, hd, 0, :, rows] = vT_all[r][hd * V_DIM:(hd + 1) * V_DIM].astype(BF16)
            q = qT_all[r][hd * HEAD_PAD:(hd + 1) * HEAD_PAD]
            nope, pair = q[:NOPE_DIM], q[NOPE_DIM:]
            qn = ((nope * col_rms(nope)) * qn_gain) * scale
            t = ((pair * col_rms(pair[:ROPE_DIM])) * qr_gain) * csT
            qr = (t[:ROPE_DIM] + t[ROPE_DIM:]) * scale
            qT_ref[0, hd, 0:NOPE_DIM, rows] = qn.astype(BF16)
            qT_ref[0, hd, NOPE_DIM:NOPE_DIM + ROPE_DIM, rows] = qr.astype(BF16)
            qT_ref[0, hd, NOPE_DIM + ROPE_DIM:, rows] = jnp.zeros((HEAD_PAD - NOPE_DIM - ROPE_DIM, sub_rows), BF16)


def _const_spec(shape):
    nd = len(shape)
    return pl.BlockSpec(shape, lambda *_: (0,) * nd, pipeline_mode=pl.Buffered(1))


def _pre(x, mod, lw, cs, *, tm):
    b, n, d = x.shape
    n_heads = lw["n_heads"]
    side = lw["side"]
    nt = n // tm
    mod_rows = mod.shape[0]
    mod_map = (lambda bi, j: (bi, 0, 0)) if mod_rows > 1 else (lambda bi, j: (0, 0, 0))
    kern = functools.partial(_pre_kernel, n_heads=n_heads, q_rank=lw["q_rank"],
                             kv_rank=lw["kv_rank"], side=side, scale=lw["scale"])
    consts = [lw["norm1_g"], lw["w_in"], lw["qc_g"], lw["kvc_g"], lw["w_qT"], lw["w_kn"], lw["w_vT"],
              lw["qn_g"], lw["qr_g"], lw["kn_g"], lw["kr_g"]]
    cs, csT = cs
    return pl.pallas_call(
        kern,
        out_shape=(
            jax.ShapeDtypeStruct((b, n_heads, HEAD_PAD, n), BF16),
            jax.ShapeDtypeStruct((b, n_heads, n, HEAD_PAD), BF16),
            jax.ShapeDtypeStruct((b, n_heads, nt, V_DIM, tm), BF16),
            jax.ShapeDtypeStruct((b, n, 3 * side), F32),
        ),
        grid=(b, nt),
        in_specs=[pl.BlockSpec((1, tm, d), lambda bi, j: (bi, j, 0)),
                  pl.BlockSpec((1, N_MOD, d), mod_map)]
                 + [_const_spec(c.shape) for c in consts]
                 + [pl.BlockSpec((tm, LANES), lambda bi, j: (j, 0)),
                    pl.BlockSpec((LANES, tm), lambda bi, j: (0, j))],
        out_specs=(
            pl.BlockSpec((1, n_heads, HEAD_PAD, tm), lambda bi, j: (bi, 0, 0, j)),
            pl.BlockSpec((1, n_heads, tm, HEAD_PAD), lambda bi, j: (bi, 0, j, 0)),
            pl.BlockSpec((1, n_heads, 1, V_DIM, tm), lambda bi, j: (bi, 0, j, 0, 0)),
            pl.BlockSpec((1, tm, 3 * side), lambda bi, j: (bi, j, 0)),
        ),
        compiler_params=pltpu.CompilerParams(
            dimension_semantics=("arbitrary", "arbitrary"), vmem_limit_bytes=VMEM_LIMIT),
        name="pre_proj",
    )(x, mod, *consts, cs, csT)


def _col_reduce(x, op):
    rows = x.shape[0]
    while rows % (2 * SUBLANES) == 0:
        rows //= 2
        x = op(x[:rows], x[rows:])
    if op is jnp.add:
        return jnp.sum(x, axis=0, keepdims=True)
    return jnp.max(x, axis=0, keepdims=True)


def _attn_kernel(qT_ref, *refs, n_chunks, tq):
    o_ref = refs[-1]
    n_tiles = qT_ref.shape[3] // tq
    items = [(t, refs[2 * src], refs[2 * src + 1], c)
             for t in range(n_tiles) for src, nc in enumerate(n_chunks) for c in range(nc)]
    per_tile = len(items) // n_tiles

    n = len(items)
    ahead = min(ATTN_LOOKAHEAD, n)
    pending = []
    m_run = None

    def issue(i):
        nonlocal m_run
        t, k_ref, _, c = items[i]
        s = _dot(k_ref[0, 0, c], qT_ref[0, 0, :, t * tq:(t + 1) * tq])
        cm = _col_reduce(s, jnp.maximum)
        if i % per_tile == 0:
            m_new, alpha = cm, None
        else:
            m_new = jnp.maximum(m_run, cm)
            alpha = jnp.exp2(m_run - m_new)
        pending.append((s, m_new, alpha))
        m_run = m_new

    for i in range(ahead):
        issue(i)
    l = acc = None
    for i, (t, _, vT_ref, c) in enumerate(items):
        if i + ahead < n:
            issue(i + ahead)
        s, m, alpha = pending.pop(0)
        p = jnp.exp2(s - m)
        p_sum = _col_reduce(p, jnp.add)
        pv = _dot(vT_ref[0, 0, c], p.astype(BF16))
        if alpha is None:
            l, acc = p_sum, pv
        else:
            l, acc = alpha * l + p_sum, alpha * acc + pv
        if (i + 1) % per_tile == 0:
            o_ref[0, t * tq:(t + 1) * tq, :] = (acc * (1.0 / l)).T.astype(BF16)


def _attn_fixed_shift_kernel(qT_ref, *refs, n_chunks, tq):
    o_ref, check_ref = refs[-2], refs[-1]
    n_tiles = qT_ref.shape[3] // tq
    chunks = [(refs[2 * src], refs[2 * src + 1], c)
              for src, nc in enumerate(n_chunks) for c in range(nc)]
    per_tile = len(chunks)
    items = [(t, i) for t in range(n_tiles) for i in range(per_tile)]
    n = len(items)
    ahead = min(FIXED_SHIFT_LOOKAHEAD, n)
    pack_rows = 2 * SUBLANES
    first_bias_row = lax.broadcasted_iota(jnp.int32, (pack_rows, 1), 0) == 0
    q_shifted = {}
    pending = []

    def issue(j):
        t, i = items[j]
        k_ref, _, c = chunks[i]
        if i == 0:
            qT = qT_ref[0, 0, :, t * tq:(t + 1) * tq]
            s = _dot(k_ref[0, 0, c], qT)
            shift = _col_reduce(s, jnp.maximum).astype(BF16)
            bias_rows = qT[BIAS_FEATURE:BIAS_FEATURE + pack_rows]
            q_shifted[t] = jnp.concatenate(
                [qT[:BIAS_FEATURE], jnp.where(first_bias_row, -shift, bias_rows),
                 qT[BIAS_FEATURE + pack_rows:]], axis=0)
            pending.append(s - shift.astype(F32))
        else:
            pending.append(_dot(k_ref[0, 0, c], q_shifted[t]))

    for j in range(ahead):
        issue(j)
    l_part = acc = None
    for j, (t, i) in enumerate(items):
        if j + ahead < n:
            issue(j + ahead)
        p = jnp.exp2(pending.pop(0))
        p_part = _col_partial_sum(p)
        pv = _dot(chunks[i][1][0, 0, chunks[i][2]], p.astype(BF16))
        l_part, acc = (p_part, pv) if i == 0 else (l_part + p_part, acc + pv)
        if i == per_tile - 1:
            cols = slice(t * tq, (t + 1) * tq)
            l = jnp.sum(l_part, axis=0, keepdims=True)
            o_ref[0, cols, :] = (acc * (1.0 / l)).T.astype(BF16)
            check = l * 0.0 + jnp.sum(acc * 0.0, axis=0, keepdims=True)
            check_ref[0, 0, 0, :, cols] = jnp.broadcast_to(check, (SUBLANES, tq))


def _col_partial_sum(x):
    rows = x.shape[0]
    while rows % (2 * SUBLANES) == 0:
        rows //= 2
        x = x[:rows] + x[rows:]
    return x


def _attn(qT, sources, *, tq, tiles_per_step, fixed_shift=False):
    b, n_heads, _, n = qT.shape
    tq_step = tq * tiles_per_step
    in_specs = [pl.BlockSpec((1, 1, HEAD_PAD, tq_step), lambda bi, hi, i: (bi, hi, 0, i))]
    args = [qT]
    n_chunks = []
    for k, vT in sources:
        nc, tk = k.shape[2], k.shape[3]
        n_chunks.append(nc)
        in_specs.append(pl.BlockSpec((1, 1, nc, tk, HEAD_PAD), lambda bi, hi, i: (bi, hi, 0, 0, 0)))
        in_specs.append(pl.BlockSpec((1, 1, nc, V_DIM, tk), lambda bi, hi, i: (bi, hi, 0, 0, 0)))
        args += [k, vT]
    steps = n // tq_step
    out_shape = jax.ShapeDtypeStruct((b, n, n_heads * V_DIM), BF16)
    out_specs = pl.BlockSpec((1, tq_step, V_DIM), lambda bi, hi, i: (bi, i, hi))
    body = _attn_kernel
    if fixed_shift:
        body = _attn_fixed_shift_kernel
        out_shape = (out_shape, jax.ShapeDtypeStruct((b, n_heads, steps, SUBLANES, tq_step), F32))
        out_specs = (out_specs,
                     pl.BlockSpec((1, 1, 1, SUBLANES, tq_step), lambda bi, hi, i: (bi, hi, i, 0, 0)))
    return pl.pallas_call(
        functools.partial(body, n_chunks=tuple(n_chunks), tq=tq),
        out_shape=out_shape,
        grid=(b, n_heads, steps),
        in_specs=in_specs,
        out_specs=out_specs,
        compiler_params=pltpu.CompilerParams(
            dimension_semantics=("arbitrary", "arbitrary", "arbitrary"),
            vmem_limit_bytes=VMEM_LIMIT),
        name="flash_attn_fixed_shift" if fixed_shift else "flash_attn",
    )(*args)


def _halo_specs(tm, width, nt):
    per = tm // HALO

    def prev_map(bi, j):
        return (bi, jnp.maximum(j * per - 1, 0), 0)

    def next_map(bi, j):
        return (bi, jnp.minimum((j + 1) * per, nt * per - 1), 0)

    return [pl.BlockSpec((1, tm, width), lambda bi, j: (bi, j, 0)),
            pl.BlockSpec((1, HALO, width), prev_map),
            pl.BlockSpec((1, HALO, width), next_map)]


def _shift_rows(a, s):
    return pltpu.roll(a, s % a.shape[0], 0)


def _mix_kernel(x_ref, attn_ref, rest_ref, rprev_ref, rnext_ref, mod_ref, pool_w_ref,
                pool_scale_ref, sconv_ref, wo_ref, o_ref, *, side, attn_width, nt):
    j = pl.program_id(1)
    tm = x_ref.shape[1]
    seq_len = tm * nt
    rest = rest_ref[0]
    prev = jnp.where(j == 0, 0.0, rprev_ref[0])
    nxt = jnp.where(j == nt - 1, 0.0, rnext_ref[0])
    ext = jnp.concatenate([prev, rest, nxt], axis=0)
    main = slice(HALO, HALO + tm)

    pe = ext[:, 0:side]
    w2 = pe + _shift_rows(pe, 1)
    w4 = _shift_rows(w2, 1) + _shift_rows(w2, -1)
    w4_hi = w4[:, side // 2:]
    w8 = _shift_rows(w4_hi, 2) + _shift_rows(w4_hi, -2)
    w16 = _shift_rows(w8, 4) + _shift_rows(w8, -4)
    n_groups = len(POOL_WINDOWS)
    grp = lax.broadcasted_iota(jnp.int32, (1, side // 2), 1) // (side // n_groups)
    total = jnp.concatenate([jnp.where(grp == 0, w2[main, :side // 2], w4[main, :side // 2]),
                             jnp.where(grp == 0, w8[main], w16[main])], axis=1)
    grp = lax.broadcasted_iota(jnp.int32, (1, side), 1) // (side // n_groups)
    half = jnp.where(grp == 0, POOL_WINDOWS[0] // 2, jnp.where(grp == 1, POOL_WINDOWS[1] // 2,
                     jnp.where(grp == 2, POOL_WINDOWS[2] // 2, POOL_WINDOWS[3] // 2)))
    t = j * tm + lax.broadcasted_iota(jnp.int32, (tm, 1), 0)
    count = jnp.minimum(t + half, seq_len) - jnp.maximum(t - half, 0)
    diff = total / count.astype(F32) - rest[:, 0:side]
    pool_out = _dot(diff.astype(BF16), pool_w_ref[...]) * pool_scale_ref[...]

    ue = ext[:, 2 * side:3 * side]
    conv = (_shift_rows(ue, 1)[main] * sconv_ref[0:1, :] + rest[:, 2 * side:3 * side] * sconv_ref[1:2, :]
            + _shift_rows(ue, -1)[main] * sconv_ref[2:3, :])
    conv_out = rest[:, side:2 * side] * conv

    mix = (_dot(attn_ref[0], wo_ref[0:attn_width, :])
           + _dot(pool_out.astype(BF16), wo_ref[attn_width:attn_width + side, :])
           + _dot(conv_out.astype(BF16), wo_ref[attn_width + side:, :]))
    o_ref[0] = x_ref[0] + mod_ref[0, 2:3, :] * mix


def _mix(x, attn, rest, mod, lw, *, tm):
    b, n, d = x.shape
    side = lw["side"]
    attn_width = attn.shape[-1]
    mod_rows = mod.shape[0]
    mod_map = (lambda bi, j: (bi, 0, 0)) if mod_rows > 1 else (lambda bi, j: (0, 0, 0))
    consts = [lw["pool_w"], lw["pool_scale"], lw["sconv_w"], lw["w_o"]]
    nt = n // tm
    return pl.pallas_call(
        functools.partial(_mix_kernel, side=side, attn_width=attn_width, nt=nt),
        out_shape=jax.ShapeDtypeStruct((b, n, d), F32),
        grid=(b, nt),
        in_specs=[pl.BlockSpec((1, tm, d), lambda bi, j: (bi, j, 0)),
                  pl.BlockSpec((1, tm, attn_width), lambda bi, j: (bi, j, 0))]
                 + _halo_specs(tm, 3 * side, nt)
                 + [pl.BlockSpec((1, N_MOD, d), mod_map)]
                 + [_const_spec(c.shape) for c in consts],
        out_specs=pl.BlockSpec((1, tm, d), lambda bi, j: (bi, j, 0)),
        compiler_params=pltpu.CompilerParams(
            dimension_semantics=("arbitrary", "arbitrary"), vmem_limit_bytes=VMEM_LIMIT),
        name="mixer_out",
    )(x, attn, rest, rest, rest, mod, *consts)


def _ffn_kernel(x_ref, xprev_ref, xnext_ref, mod_ref, ng_ref, wup_ref, wconv_ref, wd_ref,
                o_ref, act_ref, *, nt):
    j = pl.program_id(1)
    tm = x_ref.shape[1]
    d_ff = wd_ref.shape[0]
    cw = FF_CHUNK
    nc = d_ff // cw
    x = x_ref[0]
    xe = jnp.concatenate([xprev_ref[0], x, xnext_ref[0]], axis=0)
    he = _rms(xe) * ng_ref[...]
    he = he * (1.0 + mod_ref[0, 4:5, :]) + mod_ref[0, 3:4, :]
    main = slice(HALO, HALO + tm)
    hm = he[main].astype(BF16)
    he = he.astype(BF16)

    def up(c):
        return (_dot(he, wup_ref[:, c * cw:(c + 1) * cw]),
                _dot(hm, wup_ref[:, d_ff + c * cw:d_ff + (c + 1) * cw]))

    nxt = up(0)
    for c in range(nc):
        g, u = nxt
        if c + 1 < nc:
            nxt = up(c + 1)
        g = jnp.concatenate([jnp.where(j > 0, g[:HALO], 0.0), g[main],
                             jnp.where(j < nt - 1, g[HALO + tm:], 0.0)], axis=0)
        w = wconv_ref[:, c * cw:(c + 1) * cw]
        cv = (_shift_rows(g, 1)[main] * w[0:1, :] + g[main] * w[1:2, :]
              + _shift_rows(g, -1)[main] * w[2:3, :])
        act_ref[:, c * cw:(c + 1) * cw] = ((cv * _sigmoid(cv)) * u).astype(BF16)

    o_ref[0] = x + mod_ref[0, 5:6, :] * _dot(act_ref[...], wd_ref[...])


def _ffn(x, mod, lw, *, tm):
    b, n, d = x.shape
    mod_rows = mod.shape[0]
    mod_map = (lambda bi, j: (bi, 0, 0)) if mod_rows > 1 else (lambda bi, j: (0, 0, 0))
    consts = [lw["norm2_g"], lw["ff_up"], lw["ff_conv"], lw["ff_down"]]
    nt = n // tm
    return pl.pallas_call(
        functools.partial(_ffn_kernel, nt=nt),
        out_shape=jax.ShapeDtypeStruct((b, n, d), F32),
        grid=(b, nt),
        in_specs=_halo_specs(tm, d, nt)
                 + [pl.BlockSpec((1, N_MOD, d), mod_map)]
                 + [_const_spec(c.shape) for c in consts],
        out_specs=pl.BlockSpec((1, tm, d), lambda bi, j: (bi, j, 0)),
        scratch_shapes=[pltpu.VMEM((tm, lw["ff_down"].shape[0]), BF16)],
        compiler_params=pltpu.CompilerParams(
            dimension_semantics=("arbitrary", "arbitrary"), vmem_limit_bytes=VMEM_LIMIT),
        name="conv_ffn",
    )(x, x, x, mod, *consts)


def _half_swap(n):
    quarter = ROPE_DIM // 4
    idx = np.arange(n).reshape(-1, 2, quarter)
    return idx[:, ::-1, :].reshape(-1)


def _rot_columns(w):
    quarter = ROPE_DIM // 4
    sign = np.tile(np.concatenate([-np.ones(quarter), np.ones(quarter)]), ROPE_DIM // (2 * quarter))
    return w[:, _half_swap(ROPE_DIM)] * jnp.asarray(sign, w.dtype)


def _layer_weights(i, p):
    d = p["w_in"].shape[1]
    q_rank = p["qc_g"].shape[1]
    kv_rank = p["kvc_g"].shape[1]
    side = p["pool_scale"].shape[1]
    qk_dim = NOPE_DIM + ROPE_DIM
    n_heads = p["w_q_up"].shape[2] // qk_dim
    w_in = p["w_in"][i]
    o_kr = q_rank + kv_rank
    w_in_ext = jnp.concatenate(
        [w_in[:, :o_kr + ROPE_DIM], _rot_columns(w_in[:, o_kr:o_kr + ROPE_DIM]), w_in[:, o_kr + ROPE_DIM:]],
        axis=1)
    wq = p["w_q_up"][i].reshape(q_rank, n_heads, qk_dim)
    wq_ext = jnp.concatenate(
        [wq, jax.vmap(_rot_columns, in_axes=1, out_axes=1)(wq[:, :, NOPE_DIM:])], axis=2)
    swap = _half_swap(ROPE_DIM)
    pool_w = p["pool_w"][i]
    n_groups, pool_ch = pool_w.shape[0], pool_w.shape[1]
    pool_bd = jnp.zeros((side, side), F32)
    for g in range(n_groups):
        pool_bd = pool_bd.at[g * pool_ch:(g + 1) * pool_ch, g * pool_ch:(g + 1) * pool_ch].set(pool_w[g])

    def row(v):
        return v.reshape(1, -1)

    def lane_bcast(v):
        return jnp.broadcast_to(v[:, None], (v.shape[0], LANES))

    w_kv = p["w_kv_up"][i].reshape(kv_rank, n_heads, NOPE_DIM + V_DIM)

    return dict(
        n_heads=n_heads, q_rank=q_rank, kv_rank=kv_rank, side=side,
        scale=float(qk_dim) ** -0.5 * float(np.log2(np.e)),
        norm1_g=row(p["norm1_g"][i]), norm2_g=row(p["norm2_g"][i]),
        w_in=w_in_ext.astype(BF16), qc_g=row(p["qc_g"][i]), kvc_g=row(p["kvc_g"][i]),
        w_qT=wq_ext.reshape(q_rank, n_heads * HEAD_PAD).T.astype(BF16),
        w_kn=w_kv[:, :, :NOPE_DIM].reshape(kv_rank, n_heads * NOPE_DIM).astype(BF16),
        w_vT=w_kv[:, :, NOPE_DIM:].reshape(kv_rank, n_heads * V_DIM).T.astype(BF16),
        qn_g=lane_bcast(p["qn_g"][i]), kn_g=row(p["kn_g"][i]),
        qr_g=lane_bcast(jnp.concatenate([p["qr_g"][i], p["qr_g"][i][swap]])),
        kr_g=row(jnp.concatenate([p["kr_g"][i], p["kr_g"][i][swap]])),
        pool_w=pool_bd.astype(BF16), pool_scale=row(p["pool_scale"][i]), sconv_w=p["sconv_w"][i],
        w_o=p["w_o"][i].astype(BF16),
        ff_up=p["ff_up"][i].astype(BF16), ff_conv=p["ff_conv"][i], ff_down=p["ff_down"][i].astype(BF16),
    )


def _rope_table(n):
    rows = n // GRID_W
    row = jnp.repeat(jnp.arange(rows), GRID_W).astype(F32)
    col = jnp.tile(jnp.arange(GRID_W), rows).astype(F32)
    n_freq = ROPE_DIM // 4
    inv = ROPE_THETA ** (-jnp.arange(n_freq, dtype=F32) / n_freq)
    ar = row[:, None] * inv
    ac = col[:, None] * inv
    ang = jnp.concatenate([ar, ar, ac, ac], axis=-1)
    return jnp.concatenate([jnp.cos(ang), jnp.sin(ang)], axis=-1)


def kernel(x, c, ctx, c_ctx, ada_w, ada_b, norm1_g, norm2_g, w_in, qc_g, kvc_g, w_q_up, w_kv_up,
           qn_g, qr_g, kn_g, kr_g, pool_w, pool_scale, sconv_w, w_o, ff_up, ff_conv, ff_down):
    params = dict(norm1_g=norm1_g, norm2_g=norm2_g, w_in=w_in, qc_g=qc_g, kvc_g=kvc_g,
                  w_q_up=w_q_up, w_kv_up=w_kv_up, qn_g=qn_g, qr_g=qr_g, kn_g=kn_g, kr_g=kr_g,
                  pool_w=pool_w, pool_scale=pool_scale, sconv_w=sconv_w, w_o=w_o, ff_up=ff_up,
                  ff_conv=ff_conv, ff_down=ff_down)
    depth = ada_w.shape[0]
    b, n, d = x.shape
    n_ctx = ctx.shape[1]
    tm = min(ROW_TILE, n)
    tq = min(Q_TILE, n)

    mod_rows = -(-(b + 1) // SUBLANES) * SUBLANES
    cvec = jnp.zeros((mod_rows, d), F32).at[:b].set(c).at[b].set(c_ctx)
    mods = _adaln(cvec, ada_w, ada_b).reshape(depth, mod_rows, N_MOD, d)

    cs_x = _rope_table(n)
    cs_c = jnp.concatenate([jnp.ones((n_ctx, ROPE_DIM), F32), jnp.zeros((n_ctx, ROPE_DIM), F32)], axis=-1)
    cs_x, cs_c = (cs_x, cs_x.T), (cs_c, cs_c.T)

    for i in range(depth):
        lw = _layer_weights(i, params)
        mod_x = mods[i, :b]
        mod_c = mods[i, b:b + 1]
        qT_x, k_x, vT_x, rest_x = _pre(x, mod_x, lw, cs_x, tm=tm)
        qT_c, k_c, vT_c, rest_c = _pre(ctx, mod_c, lw, cs_c, tm=n_ctx)
        nh = lw["n_heads"]
        src_x = (k_x.reshape(b, nh, n // tm, tm, HEAD_PAD), vT_x)
        src_c = (k_c.reshape(b, nh, 1, n_ctx, HEAD_PAD), vT_c)
        tiles = min(Q_TILES_PER_STEP, n // tq)
        attn_x, check = _attn(qT_x, [src_x, src_c], tq=tq, tiles_per_step=tiles, fixed_shift=True)
        attn_x = lax.cond(
            jnp.any(jnp.isnan(check)),
            lambda q, sx, sc, a: _attn(q, [sx, sc], tq=tq, tiles_per_step=tiles),
            lambda q, sx, sc, a: a,
            qT_x, src_x, src_c, attn_x)
        x = _mix(x, attn_x, rest_x, mod_x, lw, tm=tm)
        x = _ffn(x, mod_x, lw, tm=tm)
        if i < depth - 1:
            attn_c = _attn(qT_c, [src_c], tq=min(tq, n_ctx), tiles_per_step=1)
            ctx = _mix(ctx, attn_c, rest_c, mod_c, lw, tm=n_ctx)
            ctx = _ffn(ctx, mod_c, lw, tm=n_ctx)
    return x
```

```python
import functools

import jax
import jax.numpy as jnp
import numpy as np
from jax import lax
from jax.experimental import pallas as pl
from jax.experimental.pallas import tpu as pltpu

F32 = jnp.float32
BF16 = jnp.bfloat16

GRID_W = 64
NOPE_DIM = 128
ROPE_DIM = 64
V_DIM = 128
ROPE_THETA = 10000.0
POOL_WINDOWS = (2, 4, 8, 16)
CONV_K = 3
EPS = 1e-6
N_MOD = 6

LANES = 128
SUBLANES = 8
MXU_DIM = 256
HEAD_PAD = MXU_DIM
BIAS_FEATURE = NOPE_DIM + ROPE_DIM
HALO = SUBLANES
VMEM_LIMIT = 52 * 1024 * 1024

ROW_TILE = 512
Q_TILE = 256
Q_TILES_PER_STEP = 8
FF_CHUNK = 256
ATTN_LOOKAHEAD = 4
FIXED_SHIFT_LOOKAHEAD = 1


def _rms(x, width=None):
    width = x.shape[-1] if width is None else width
    return x * lax.rsqrt(jnp.sum(x * x, axis=-1, keepdims=True) * (1.0 / width) + EPS)


def _sigmoid(x):
    return 1.0 / (1.0 + jnp.exp(-x))


def _dot(a, b):
    return jnp.dot(a, b, preferred_element_type=F32)


def _adaln_kernel(c_ref, w_ref, b_ref, o_ref):
    c = c_ref[...]
    s = (c * _sigmoid(c)).astype(BF16)
    o_ref[0] = _dot(s, w_ref[0].astype(BF16)) + b_ref[0]


def _adaln(cvec, ada_w, ada_b):
    depth, d, width = ada_w.shape
    rows = cvec.shape[0]
    tn = width // 4
    return pl.pallas_call(
        _adaln_kernel,
        out_shape=jax.ShapeDtypeStruct((depth, rows, width), F32),
        grid=(depth, width // tn),
        in_specs=[
            pl.BlockSpec((rows, d), lambda i, n: (0, 0)),
            pl.BlockSpec((1, d, tn), lambda i, n: (i, 0, n)),
            pl.BlockSpec((1, 1, tn), lambda i, n: (i, 0, n)),
        ],
        out_specs=pl.BlockSpec((1, rows, tn), lambda i, n: (i, 0, n)),
        compiler_params=pltpu.CompilerParams(
            dimension_semantics=("arbitrary", "arbitrary"), vmem_limit_bytes=VMEM_LIMIT),
        name="adaln",
    )(cvec, ada_w, ada_b.reshape(depth, 1, width))


def _pre_kernel(x_ref, mod_ref, ng_ref, w_in_ref, qcg_ref, kvcg_ref, wqT_ref, wkn_ref, wvT_ref,
                qng_ref, qrg_ref, kng_ref, krg_ref, cs_ref, csT_ref,
                qT_ref, k_ref, vT_ref, rest_ref, *, n_heads, q_rank, kv_rank, side, scale):
    tm = x_ref.shape[1]
    n_sub = 2 if tm % (2 * MXU_DIM) == 0 else 1
    sub_rows = tm // n_sub
    subs = [slice(r * sub_rows, (r + 1) * sub_rows) for r in range(n_sub)]
    o_kv = q_rank
    o_kr = o_kv + kv_rank
    o_pool = o_kr + 2 * ROPE_DIM
    lane = lax.broadcasted_iota(jnp.int32, (1, LANES), 1)
    keep = (lane < ROPE_DIM).astype(F32)
    bias_lane = (lane == BIAS_FEATURE - NOPE_DIM).astype(F32)

    def key_rope(tile, rows):
        t = (_rms(tile) * krg_ref[...]) * cs_ref[rows, :]
        return (t + pltpu.roll(t, ROPE_DIM, 1)) * keep + bias_lane

    def lane_tiled(ref):
        return jnp.tile(ref[...], (1, sub_rows // LANES))

    def col_rms(a):
        return lax.rsqrt(jnp.sum(a * a, axis=0, keepdims=True) * (1.0 / a.shape[0]) + EPS)

    hb, lat = [], []
    for rows in subs:
        h = _rms(x_ref[0, rows, :]) * ng_ref[...]
        h = h * (1.0 + mod_ref[0, 1:2, :]) + mod_ref[0, 0:1, :]
        hb.append(h.astype(BF16))
        lat.append(_dot(hb[-1], w_in_ref[:, 0:o_pool]))

    for rows, hs in zip(subs, hb):
        rest_ref[0, rows, 0:2 * side] = _dot(hs, w_in_ref[:, o_pool:o_pool + 2 * side])
        gc_cv = _dot(hs, w_in_ref[:, o_pool + 2 * side:o_pool + 4 * side])
        rest_ref[0, rows, 2 * side:3 * side] = gc_cv[:, :side] * gc_cv[:, side:]

    qcT = [(_rms(t[:, 0:q_rank]) * qcg_ref[...]).T.astype(BF16) for t in lat]
    kvc = [_rms(t[:, o_kv:o_kr]) * kvcg_ref[...] for t in lat]
    kvcT = [t.T.astype(BF16) for t in kvc]
    kvc = [t.astype(BF16) for t in kvc]
    qT_all = [_dot(wqT_ref[...], t) for t in qcT]
    kn_all = [_dot(t, wkn_ref[...]) for t in kvc]
    vT_all = [_dot(wvT_ref[...], t) for t in kvcT]
    qn_gain, qr_gain = lane_tiled(qng_ref), lane_tiled(qrg_ref)
    for r, rows in enumerate(subs):
        kr = key_rope(lat[r][:, o_kr:o_pool], rows).astype(BF16)
        csT = csT_ref[:, rows]
        for hd in range(n_heads):
            kn = kn_all[r][:, hd * NOPE_DIM:(hd + 1) * NOPE_DIM]
            k_ref[0, hd, rows, :NOPE_DIM] = (_rms(kn) * kng_ref[...]).astype(BF16)
            k_ref[0, hd, rows, NOPE_DIM:] = kr
            vT_ref[0, hd, 0, :, rows] = vT_all[r][hd * V_DIM:(hd + 1) * V_DIM].astype(BF16)
            q = qT_all[r][hd * HEAD_PAD:(hd + 1) * HEAD_PAD]
            nope, pair = q[:NOPE_DIM], q[NOPE_DIM:]
            qn = ((nope * col_rms(nope)) * qn_gain) * scale
            t = ((pair * col_rms(pair[:ROPE_DIM])) * qr_gain) * csT
            qr = (t[:ROPE_DIM] + t[ROPE_DIM:]) * scale
            qT_ref[0, hd, 0:NOPE_DIM, rows] = qn.astype(BF16)
            qT_ref[0, hd, NOPE_DIM:NOPE_DIM + ROPE_DIM, rows] = qr.astype(BF16)
            qT_ref[0, hd, NOPE_DIM + ROPE_DIM:, rows] = jnp.zeros((HEAD_PAD - NOPE_DIM - ROPE_DIM, sub_rows), BF16)


def _const_spec(shape):
    nd = len(shape)
    return pl.BlockSpec(shape, lambda *_: (0,) * nd, pipeline_mode=pl.Buffered(1))


def _pre(x, mod, lw, cs, *, tm):
    b, n, d = x.shape
    n_heads = lw["n_heads"]
    side = lw["side"]
    nt = n // tm
    mod_rows = mod.shape[0]
    mod_map = (lambda bi, j: (bi, 0, 0)) if mod_rows > 1 else (lambda bi, j: (0, 0, 0))
    kern = functools.partial(_pre_kernel, n_heads=n_heads, q_rank=lw["q_rank"],
                             kv_rank=lw["kv_rank"], side=side, scale=lw["scale"])
    consts = [lw["norm1_g"], lw["w_in"], lw["qc_g"], lw["kvc_g"], lw["w_qT"], lw["w_kn"], lw["w_vT"],
              lw["qn_g"], lw["qr_g"], lw["kn_g"], lw["kr_g"]]
    cs, csT = cs
    return pl.pallas_call(
        kern,
        out_shape=(
            jax.ShapeDtypeStruct((b, n_heads, HEAD_PAD, n), BF16),
            jax.ShapeDtypeStruct((b, n_heads, n, HEAD_PAD), BF16),
            jax.ShapeDtypeStruct((b, n_heads, nt, V_DIM, tm), BF16),
            jax.ShapeDtypeStruct((b, n, 3 * side), F32),
        ),
        grid=(b, nt),
        in_specs=[pl.BlockSpec((1, tm, d), lambda bi, j: (bi, j, 0)),
                  pl.BlockSpec((1, N_MOD, d), mod_map)]
                 + [_const_spec(c.shape) for c in consts]
                 + [pl.BlockSpec((tm, LANES), lambda bi, j: (j, 0)),
                    pl.BlockSpec((LANES, tm), lambda bi, j: (0, j))],
        out_specs=(
            pl.BlockSpec((1, n_heads, HEAD_PAD, tm), lambda bi, j: (bi, 0, 0, j)),
            pl.BlockSpec((1, n_heads, tm, HEAD_PAD), lambda bi, j: (bi, 0, j, 0)),
            pl.BlockSpec((1, n_heads, 1, V_DIM, tm), lambda bi, j: (bi, 0, j, 0, 0)),
            pl.BlockSpec((1, tm, 3 * side), lambda bi, j: (bi, j, 0)),
        ),
        compiler_params=pltpu.CompilerParams(
            dimension_semantics=("arbitrary", "arbitrary"), vmem_limit_bytes=VMEM_LIMIT),
        name="pre_proj",
    )(x, mod, *consts, cs, csT)


def _col_reduce(x, op):
    rows = x.shape[0]
    while rows % (2 * SUBLANES) == 0:
        rows //= 2
        x = op(x[:rows], x[rows:])
    if op is jnp.add:
        return jnp.sum(x, axis=0, keepdims=True)
    return jnp.max(x, axis=0, keepdims=True)


def _attn_kernel(qT_ref, *refs, n_chunks, tq):
    o_ref = refs[-1]
    n_tiles = qT_ref.shape[3] // tq
    items = [(t, refs[2 * src], refs[2 * src + 1], c)
             for t in range(n_tiles) for src, nc in enumerate(n_chunks) for c in range(nc)]
    per_tile = len(items) // n_tiles

    n = len(items)
    ahead = min(ATTN_LOOKAHEAD, n)
    pending = []
    m_run = None

    def issue(i):
        nonlocal m_run
        t, k_ref, _, c = items[i]
        s = _dot(k_ref[0, 0, c], qT_ref[0, 0, :, t * tq:(t + 1) * tq])
        cm = _col_reduce(s, jnp.maximum)
        if i % per_tile == 0:
            m_new, alpha = cm, None
        else:
            m_new = jnp.maximum(m_run, cm)
            alpha = jnp.exp2(m_run - m_new)
        pending.append((s, m_new, alpha))
        m_run = m_new

    for i in range(ahead):
        issue(i)
    l = acc = None
    for i, (t, _, vT_ref, c) in enumerate(items):
        if i + ahead < n:
            issue(i + ahead)
        s, m, alpha = pending.pop(0)
        p = jnp.exp2(s - m)
        p_sum = _col_reduce(p, jnp.add)
        pv = _dot(vT_ref[0, 0, c], p.astype(BF16))
        if alpha is None:
            l, acc = p_sum, pv
        else:
            l, acc = alpha * l + p_sum, alpha * acc + pv
        if (i + 1) % per_tile == 0:
            o_ref[0, t * tq:(t + 1) * tq, :] = (acc * (1.0 / l)).T.astype(BF16)


def _attn_fixed_shift_kernel(qT_ref, *refs, n_chunks, tq):
    o_ref, check_ref = refs[-2], refs[-1]
    n_tiles = qT_ref.shape[3] // tq
    chunks = [(refs[2 * src], refs[2 * src + 1], c)
              for src, nc in enumerate(n_chunks) for c in range(nc)]
    per_tile = len(chunks)
    items = [(t, i) for t in range(n_tiles) for i in range(per_tile)]
    n = len(items)
    ahead = min(FIXED_SHIFT_LOOKAHEAD, n)
    pack_rows = 2 * SUBLANES
    first_bias_row = lax.broadcasted_iota(jnp.int32, (pack_rows, 1), 0) == 0
    q_shifted = {}
    pending = []

    def issue(j):
        t, i = items[j]
        k_ref, _, c = chunks[i]
        if i == 0:
            qT = qT_ref[0, 0, :, t * tq:(t + 1) * tq]
            s = _dot(k_ref[0, 0, c], qT)
            shift = _col_reduce(s, jnp.maximum).astype(BF16)
            bias_rows = qT[BIAS_FEATURE:BIAS_FEATURE + pack_rows]
            q_shifted[t] = jnp.concatenate(
                [qT[:BIAS_FEATURE], jnp.where(first_bias_row, -shift, bias_rows),
                 qT[BIAS_FEATURE + pack_rows:]], axis=0)
            pending.append(s - shift.astype(F32))
        else:
            pending.append(_dot(k_ref[0, 0, c], q_shifted[t]))

    for j in range(ahead):
        issue(j)
    l_part = acc = None
    for j, (t, i) in enumerate(items):
        if j + ahead < n:
            issue(j + ahead)
        p = jnp.exp2(pending.pop(0))
        p_part = _col_partial_sum(p)
        pv = _dot(chunks[i][1][0, 0, chunks[i][2]], p.astype(BF16))
        l_part, acc = (p_part, pv) if i == 0 else (l_part + p_part, acc + pv)
        if i == per_tile - 1:
            cols = slice(t * tq, (t + 1) * tq)
            l = jnp.sum(l_part, axis=0, keepdims=True)
            o_ref[0, cols, :] = (acc * (1.0 / l)).T.astype(BF16)
            check = l * 0.0 + jnp.sum(acc * 0.0, axis=0, keepdims=True)
            check_ref[0, 0, 0, :, cols] = jnp.broadcast_to(check, (SUBLANES, tq))


def _col_partial_sum(x):
    rows = x.shape[0]
    while rows % (2 * SUBLANES) == 0:
        rows //= 2
        x = x[:rows] + x[rows:]
    return x


def _attn(qT, sources, *, tq, tiles_per_step, fixed_shift=False):
    b, n_heads, _, n = qT.shape
    tq_step = tq * tiles_per_step
    in_specs = [pl.BlockSpec((1, 1, HEAD_PAD, tq_step), lambda bi, hi, i: (bi, hi, 0, i))]
    args = [qT]
    n_chunks = []
    for k, vT in sources:
        nc, tk = k.shape[2], k.shape[3]
        n_chunks.append(nc)
        in_specs.append(pl.BlockSpec((1, 1, nc, tk, HEAD_PAD), lambda bi, hi, i: (bi, hi, 0, 0, 0)))
        in_specs.append(pl.BlockSpec((1, 1, nc, V_DIM, tk), lambda bi, hi, i: (bi, hi, 0, 0, 0)))
        args += [k, vT]
    steps = n // tq_step
    out_shape = jax.ShapeDtypeStruct((b, n, n_heads * V_DIM), BF16)
    out_specs = pl.BlockSpec((1, tq_step, V_DIM), lambda bi, hi, i: (bi, i, hi))
    body = _attn_kernel
    if fixed_shift:
        body = _attn_fixed_shift_kernel
        out_shape = (out_shape, jax.ShapeDtypeStruct((b, n_heads, steps, SUBLANES, tq_step), F32))
        out_specs = (out_specs,
                     pl.BlockSpec((1, 1, 1, SUBLANES, tq_step), lambda bi, hi, i: (bi, hi, i, 0, 0)))
    return pl.pallas_call(
        functools.partial(body, n_chunks=tuple(n_chunks), tq=tq),
        out_shape=out_shape,
        grid=(b, n_heads, steps),
        in_specs=in_specs,
        out_specs=out_specs,
        compiler_params=pltpu.CompilerParams(
            dimension_semantics=("arbitrary", "arbitrary", "arbitrary"),
            vmem_limit_bytes=VMEM_LIMIT),
        name="flash_attn_fixed_shift" if fixed_shift else "flash_attn",
    )(*args)


def _halo_specs(tm, width, nt):
    per = tm // HALO

    def prev_map(bi, j):
        return (bi, jnp.maximum(j * per - 1, 0), 0)

    def next_map(bi, j):
        return (bi, jnp.minimum((j + 1) * per, nt * per - 1), 0)

    return [pl.BlockSpec((1, tm, width), lambda bi, j: (bi, j, 0)),
            pl.BlockSpec((1, HALO, width), prev_map),
            pl.BlockSpec((1, HALO, width), next_map)]


def _shift_rows(a, s):
    return pltpu.roll(a, s % a.shape[0], 0)


def _mix_kernel(x_ref, attn_ref, rest_ref, rprev_ref, rnext_ref, mod_ref, pool_w_ref,
                pool_scale_ref, sconv_ref, wo_ref, o_ref, *, side, attn_width, nt):
    j = pl.program_id(1)
    tm = x_ref.shape[1]
    seq_len = tm * nt
    rest = rest_ref[0]
    prev = jnp.where(j == 0, 0.0, rprev_ref[0])
    nxt = jnp.where(j == nt - 1, 0.0, rnext_ref[0])
    ext = jnp.concatenate([prev, rest, nxt], axis=0)
    main = slice(HALO, HALO + tm)

    pe = ext[:, 0:side]
    w2 = pe + _shift_rows(pe, 1)
    w4 = _shift_rows(w2, 1) + _shift_rows(w2, -1)
    w4_hi = w4[:, side // 2:]
    w8 = _shift_rows(w4_hi, 2) + _shift_rows(w4_hi, -2)
    w16 = _shift_rows(w8, 4) + _shift_rows(w8, -4)
    n_groups = len(POOL_WINDOWS)
    grp = lax.broadcasted_iota(jnp.int32, (1, side // 2), 1) // (side // n_groups)
    total = jnp.concatenate([jnp.where(grp == 0, w2[main, :side // 2], w4[main, :side // 2]),
                             jnp.where(grp == 0, w8[main], w16[main])], axis=1)
    grp = lax.broadcasted_iota(jnp.int32, (1, side), 1) // (side // n_groups)
    half = jnp.where(grp == 0, POOL_WINDOWS[0] // 2, jnp.where(grp == 1, POOL_WINDOWS[1] // 2,
                     jnp.where(grp == 2, POOL_WINDOWS[2] // 2, POOL_WINDOWS[3] // 2)))
    t = j * tm + lax.broadcasted_iota(jnp.int32, (tm, 1), 0)
    count = jnp.minimum(t + half, seq_len) - jnp.maximum(t - half, 0)
    diff = total / count.astype(F32) - rest[:, 0:side]
    pool_out = _dot(diff.astype(BF16), pool_w_ref[...]) * pool_scale_ref[...]

    ue = ext[:, 2 * side:3 * side]
    conv = (_shift_rows(ue, 1)[main] * sconv_ref[0:1, :] + rest[:, 2 * side:3 * side] * sconv_ref[1:2, :]
            + _shift_rows(ue, -1)[main] * sconv_ref[2:3, :])
    conv_out = rest[:, side:2 * side] * conv

    mix = (_dot(attn_ref[0], wo_ref[0:attn_width, :])
           + _dot(pool_out.astype(BF16), wo_ref[attn_width:attn_width + side, :])
           + _dot(conv_out.astype(BF16), wo_ref[attn_width + side:, :]))
    o_ref[0] = x_ref[0] + mod_ref[0, 2:3, :] * mix


def _mix(x, attn, rest, mod, lw, *, tm):
    b, n, d = x.shape
    side = lw["side"]
    attn_width = attn.shape[-1]
    mod_rows = mod.shape[0]
    mod_map = (lambda bi, j: (bi, 0, 0)) if mod_rows > 1 else (lambda bi, j: (0, 0, 0))
    consts = [lw["pool_w"], lw["pool_scale"], lw["sconv_w"], lw["w_o"]]
    nt = n // tm
    return pl.pallas_call(
        functools.partial(_mix_kernel, side=side, attn_width=attn_width, nt=nt),
        out_shape=jax.ShapeDtypeStruct((b, n, d), F32),
        grid=(b, nt),
        in_specs=[pl.BlockSpec((1, tm, d), lambda bi, j: (bi, j, 0)),
                  pl.BlockSpec((1, tm, attn_width), lambda bi, j: (bi, j, 0))]
                 + _halo_specs(tm, 3 * side, nt)
                 + [pl.BlockSpec((1, N_MOD, d), mod_map)]
                 + [_const_spec(c.shape) for c in consts],
        out_specs=pl.BlockSpec((1, tm, d), lambda bi, j: (bi, j, 0)),
        compiler_params=pltpu.CompilerParams(
            dimension_semantics=("arbitrary", "arbitrary"), vmem_limit_bytes=VMEM_LIMIT),
        name="mixer_out",
    )(x, attn, rest, rest, rest, mod, *consts)


def _ffn_kernel(x_ref, xprev_ref, xnext_ref, mod_ref, ng_ref, wup_ref, wconv_ref, wd_ref,
                o_ref, act_ref, *, nt):
    j = pl.program_id(1)
    tm = x_ref.shape[1]
    d_ff = wd_ref.shape[0]
    cw = FF_CHUNK
    nc = d_ff // cw
    x = x_ref[0]
    xe = jnp.concatenate([xprev_ref[0], x, xnext_ref[0]], axis=0)
    he = _rms(xe) * ng_ref[...]
    he = he * (1.0 + mod_ref[0, 4:5, :]) + mod_ref[0, 3:4, :]
    main = slice(HALO, HALO + tm)
    hm = he[main].astype(BF16)
    he = he.astype(BF16)

    def up(c):
        return (_dot(he, wup_ref[:, c * cw:(c + 1) * cw]),
                _dot(hm, wup_ref[:, d_ff + c * cw:d_ff + (c + 1) * cw]))

    nxt = up(0)
    for c in range(nc):
        g, u = nxt
        if c + 1 < nc:
            nxt = up(c + 1)
        g = jnp.concatenate([jnp.where(j > 0, g[:HALO], 0.0), g[main],
                             jnp.where(j < nt - 1, g[HALO + tm:], 0.0)], axis=0)
        w = wconv_ref[:, c * cw:(c + 1) * cw]
        cv = (_shift_rows(g, 1)[main] * w[0:1, :] + g[main] * w[1:2, :]
              + _shift_rows(g, -1)[main] * w[2:3, :])
        act_ref[:, c * cw:(c + 1) * cw] = ((cv * _sigmoid(cv)) * u).astype(BF16)

    o_ref[0] = x + mod_ref[0, 5:6, :] * _dot(act_ref[...], wd_ref[...])


def _ffn(x, mod, lw, *, tm):
    b, n, d = x.shape
    mod_rows = mod.shape[0]
    mod_map = (lambda bi, j: (bi, 0, 0)) if mod_rows > 1 else (lambda bi, j: (0, 0, 0))
    consts = [lw["norm2_g"], lw["ff_up"], lw["ff_conv"], lw["ff_down"]]
    nt = n // tm
    return pl.pallas_call(
        functools.partial(_ffn_kernel, nt=nt),
        out_shape=jax.ShapeDtypeStruct((b, n, d), F32),
        grid=(b, nt),
        in_specs=_halo_specs(tm, d, nt)
                 + [pl.BlockSpec((1, N_MOD, d), mod_map)]
                 + [_const_spec(c.shape) for c in consts],
        out_specs=pl.BlockSpec((1, tm, d), lambda bi, j: (bi, j, 0)),
        scratch_shapes=[pltpu.VMEM((tm, lw["ff_down"].shape[0]), BF16)],
        compiler_params=pltpu.CompilerParams(
            dimension_semantics=("arbitrary", "arbitrary"), vmem_limit_bytes=VMEM_LIMIT),
        name="conv_ffn",
    )(x, x, x, mod, *consts)


def _half_swap(n):
    quarter = ROPE_DIM // 4
    idx = np.arange(n).reshape(-1, 2, quarter)
    return idx[:, ::-1, :].reshape(-1)


def _rot_columns(w):
    quarter = ROPE_DIM // 4
    sign = np.tile(np.concatenate([-np.ones(quarter), np.ones(quarter)]), ROPE_DIM // (2 * quarter))
    return w[:, _half_swap(ROPE_DIM)] * jnp.asarray(sign, w.dtype)


def _layer_weights(i, p):
    d = p["w_in"].shape[1]
    q_rank = p["qc_g"].shape[1]
    kv_rank = p["kvc_g"].shape[1]
    side = p["pool_scale"].shape[1]
    qk_dim = NOPE_DIM + ROPE_DIM
    n_heads = p["w_q_up"].shape[2] // qk_dim
    w_in = p["w_in"][i]
    o_kr = q_rank + kv_rank
    w_in_ext = jnp.concatenate(
        [w_in[:, :o_kr + ROPE_DIM], _rot_columns(w_in[:, o_kr:o_kr + ROPE_DIM]), w_in[:, o_kr + ROPE_DIM:]],
        axis=1)
    wq = p["w_q_up"][i].reshape(q_rank, n_heads, qk_dim)
    wq_ext = jnp.concatenate(
        [wq, jax.vmap(_rot_columns, in_axes=1, out_axes=1)(wq[:, :, NOPE_DIM:])], axis=2)
    swap = _half_swap(ROPE_DIM)
    pool_w = p["pool_w"][i]
    n_groups, pool_ch = pool_w.shape[0], pool_w.shape[1]
    pool_bd = jnp.zeros((side, side), F32)
    for g in range(n_groups):
        pool_bd = pool_bd.at[g * pool_ch:(g + 1) * pool_ch, g * pool_ch:(g + 1) * pool_ch].set(pool_w[g])

    def row(v):
        return v.reshape(1, -1)

    def lane_bcast(v):
        return jnp.broadcast_to(v[:, None], (v.shape[0], LANES))

    w_kv = p["w_kv_up"][i].reshape(kv_rank, n_heads, NOPE_DIM + V_DIM)

    return dict(
        n_heads=n_heads, q_rank=q_rank, kv_rank=kv_rank, side=side,
        scale=float(qk_dim) ** -0.5 * float(np.log2(np.e)),
        norm1_g=row(p["norm1_g"][i]), norm2_g=row(p["norm2_g"][i]),
        w_in=w_in_ext.astype(BF16), qc_g=row(p["qc_g"][i]), kvc_g=row(p["kvc_g"][i]),
        w_qT=wq_ext.reshape(q_rank, n_heads * HEAD_PAD).T.astype(BF16),
        w_kn=w_kv[:, :, :NOPE_DIM].reshape(kv_rank, n_heads * NOPE_DIM).astype(BF16),
        w_vT=w_kv[:, :, NOPE_DIM:].reshape(kv_rank, n_heads * V_DIM).T.astype(BF16),
        qn_g=lane_bcast(p["qn_g"][i]), kn_g=row(p["kn_g"][i]),
        qr_g=lane_bcast(jnp.concatenate([p["qr_g"][i], p["qr_g"][i][swap]])),
        kr_g=row(jnp.concatenate([p["kr_g"][i], p["kr_g"][i][swap]])),
        pool_w=pool_bd.astype(BF16), pool_scale=row(p["pool_scale"][i]), sconv_w=p["sconv_w"][i],
        w_o=p["w_o"][i].astype(BF16),
        ff_up=p["ff_up"][i].astype(BF16), ff_conv=p["ff_conv"][i], ff_down=p["ff_down"][i].astype(BF16),
    )


def _rope_table(n):
    rows = n // GRID_W
    row = jnp.repeat(jnp.arange(rows), GRID_W).astype(F32)
    col = jnp.tile(jnp.arange(GRID_W), rows).astype(F32)
    n_freq = ROPE_DIM // 4
    inv = ROPE_THETA ** (-jnp.arange(n_freq, dtype=F32) / n_freq)
    ar = row[:, None] * inv
    ac = col[:, None] * inv
    ang = jnp.concatenate([ar, ar, ac, ac], axis=-1)
    return jnp.concatenate([jnp.cos(ang), jnp.sin(ang)], axis=-1)


def kernel(x, c, ctx, c_ctx, ada_w, ada_b, norm1_g, norm2_g, w_in, qc_g, kvc_g, w_q_up, w_kv_up,
           qn_g, qr_g, kn_g, kr_g, pool_w, pool_scale, sconv_w, w_o, ff_up, ff_conv, ff_down):
    params = dict(norm1_g=norm1_g, norm2_g=norm2_g, w_in=w_in, qc_g=qc_g, kvc_g=kvc_g,
                  w_q_up=w_q_up, w_kv_up=w_kv_up, qn_g=qn_g, qr_g=qr_g, kn_g=kn_g, kr_g=kr_g,
                  pool_w=pool_w, pool_scale=pool_scale, sconv_w=sconv_w, w_o=w_o, ff_up=ff_up,
                  ff_conv=ff_conv, ff_down=ff_down)
    depth = ada_w.shape[0]
    b, n, d = x.shape
    n_ctx = ctx.shape[1]
    tm = min(ROW_TILE, n)
    tq = min(Q_TILE, n)

    mod_rows = -(-(b + 1) // SUBLANES) * SUBLANES
    cvec = jnp.zeros((mod_rows, d), F32).at[:b].set(c).at[b].set(c_ctx)
    mods = _adaln(cvec, ada_w, ada_b).reshape(depth, mod_rows, N_MOD, d)

    cs_x = _rope_table(n)
    cs_c = jnp.concatenate([jnp.ones((n_ctx, ROPE_DIM), F32), jnp.zeros((n_ctx, ROPE_DIM), F32)], axis=-1)
    cs_x, cs_c = (cs_x, cs_x.T), (cs_c, cs_c.T)

    for i in range(depth):
        lw = _layer_weights(i, params)
        mod_x = mods[i, :b]
        mod_c = mods[i, b:b + 1]
        qT_x, k_x, vT_x, rest_x = _pre(x, mod_x, lw, cs_x, tm=tm)
        qT_c, k_c, vT_c, rest_c = _pre(ctx, mod_c, lw, cs_c, tm=n_ctx)
        nh = lw["n_heads"]
        src_x = (k_x.reshape(b, nh, n // tm, tm, HEAD_PAD), vT_x)
        src_c = (k_c.reshape(b, nh, 1, n_ctx, HEAD_PAD), vT_c)
        tiles = min(Q_TILES_PER_STEP, n // tq)
        attn_x, check = _attn(qT_x, [src_x, src_c], tq=tq, tiles_per_step=tiles, fixed_shift=True)
        attn_x = lax.cond(
            jnp.any(jnp.isnan(check)),
            lambda q, sx, sc, a: _attn(q, [sx, sc], tq=tq, tiles_per_step=tiles),
            lambda q, sx, sc, a: a,
            qT_x, src_x, src_c, attn_x)
        x = _mix(x, attn_x, rest_x, mod_x, lw, tm=tm)
        x = _ffn(x, mod_x, lw, tm=tm)
        if i < depth - 1:
            attn_c = _attn(qT_c, [src_c], tq=min(tq, n_ctx), tiles_per_step=1)
            ctx = _mix(ctx, attn_c, rest_c, mod_c, lw, tm=n_ctx)
            ctx = _ffn(ctx, mod_c, lw, tm=n_ctx)
    return x
```
